```python
import jax, jax.numpy as jnp
from jax import lax
import numpy as np

D_MODEL = 1024
BATCH = 4
SEQ = 8192
DEPTH = 1

GRID_W = 64
CTX_LEN = 256
N_FOURIER_GROUPS = 4
FOURIER_GROUP_DIM = 128
FOURIER_WIDTH = N_FOURIER_GROUPS * FOURIER_GROUP_DIM
NA_HEADS = 8
HEAD_DIM = 64
NA_WIDTH = NA_HEADS * HEAD_DIM
IN_WIDTH = FOURIER_WIDTH + 3 * NA_WIDTH
WIN_ROWS = 8
WIN_COLS = 16
N_BRANCHES = 2
D_FF = 4 * D_MODEL
ROPE_THETA = 10000.0
NORM_EPS = 1e-6
N_MOD = 6

kernel_name = "hybrid_fourier_natten_dit_block"


def rms_norm(x, g):
    xf = x.astype(jnp.float32)
    y = xf * lax.rsqrt(jnp.mean(xf * xf, axis=-1, keepdims=True) + NORM_EPS)
    return (y * g.astype(jnp.float32)).astype(x.dtype)


def modulate(h, shift, scale):
    return h * (1 + scale) + shift


def ada_mod(cond, w, b):
    m = jax.nn.silu(cond) @ w + b
    return jnp.split(m, N_MOD, axis=-1)


def split_heads(a):
    return a.reshape(a.shape[0], a.shape[1], NA_HEADS, HEAD_DIM)


def split_projection(p):
    f = p[..., :FOURIER_WIDTH]
    q = p[..., FOURIER_WIDTH:FOURIER_WIDTH + NA_WIDTH]
    k = p[..., FOURIER_WIDTH + NA_WIDTH:FOURIER_WIDTH + 2 * NA_WIDTH]
    v = p[..., FOURIER_WIDTH + 2 * NA_WIDTH:]
    return f, split_heads(q), split_heads(k), split_heads(v)


def rope_1d(x, cos, sin):
    m = x.shape[-1] // 2
    x1, x2 = x[..., :m], x[..., m:]
    cos = cos[None, :, None, :]
    sin = sin[None, :, None, :]
    return jnp.concatenate([x1 * cos - x2 * sin, x1 * sin + x2 * cos], axis=-1)


def axial_rope(x):
    n_tok = x.shape[1]
    t = jnp.arange(n_tok)
    row = (t // GRID_W).astype(jnp.float32)
    col = (t % GRID_W).astype(jnp.float32)
    axis_dim = HEAD_DIM // 2
    inv_freq = ROPE_THETA ** (-jnp.arange(0, axis_dim, 2, dtype=jnp.float32) / axis_dim)
    ang_r = row[:, None] * inv_freq
    ang_c = col[:, None] * inv_freq
    xf = x.astype(jnp.float32)
    xr = rope_1d(xf[..., :axis_dim], jnp.cos(ang_r), jnp.sin(ang_r))
    xc = rope_1d(xf[..., axis_dim:], jnp.cos(ang_c), jnp.sin(ang_c))
    return jnp.concatenate([xr, xc], axis=-1).astype(x.dtype)


def fourier_mix(u):
    b, n, _ = u.shape
    ug = u.astype(jnp.float32).reshape(b, n, N_FOURIER_GROUPS, FOURIER_GROUP_DIM)
    y = jnp.fft.fft2(ug, axes=(1, 3), norm="ortho").real
    return y.reshape(b, n, FOURIER_WIDTH).astype(u.dtype)


def neighbourhood_attention(q, k, v, k_ctx, v_ctx, rpb):
    b, s, h, dh = q.shape
    rows = s // GRID_W
    kr = min(WIN_ROWS, rows)
    kc = WIN_COLS
    n_loc = kr * kc
    scale = dh ** -0.5
    qg = q.reshape(b, rows, GRID_W, h, dh)
    kg = k.reshape(b, rows, GRID_W, h, dh)
    vg = v.reshape(b, rows, GRID_W, h, dh)
    r = jnp.arange(rows)
    cc = jnp.arange(GRID_W)
    row_start = jnp.clip(r - kr // 2, 0, rows - kr)
    col_start = jnp.clip(cc - kc // 2, 0, GRID_W - kc)
    col_idx = col_start[:, None] + jnp.arange(kc)
    dc_idx = col_idx - cc[:, None] + (WIN_COLS - 1)
    dr_idx = row_start[:, None] + jnp.arange(kr) - r[:, None] + (WIN_ROWS - 1)

    def one_row(args):
        q_row, rs, dr = args
        k_rows = lax.dynamic_slice_in_dim(kg, rs, kr, axis=1)
        v_rows = lax.dynamic_slice_in_dim(vg, rs, kr, axis=1)
        k_win = k_rows[:, :, col_idx]
        v_win = v_rows[:, :, col_idx]
        bias = rpb[:, dr][:, :, dc_idx].transpose(0, 2, 1, 3)
        s_loc = (jnp.einsum('bqhd,brqchd->bhqrc', q_row, k_win).astype(jnp.float32) * scale
                 + bias[None].astype(jnp.float32))
        s_ctx = jnp.einsum('bqhd,bkhd->bhqk', q_row, k_ctx).astype(jnp.float32) * scale
        sc = jnp.concatenate([s_loc.reshape(b, h, GRID_W, n_loc), s_ctx], axis=-1)
        p = jax.nn.softmax(sc, axis=-1).astype(v.dtype)
        p_loc = p[..., :n_loc].reshape(b, h, GRID_W, kr, kc)
        p_ctx = p[..., n_loc:]
        return (jnp.einsum('bhqrc,brqchd->bqhd', p_loc, v_win)
                + jnp.einsum('bhqk,bkhd->bqhd', p_ctx, v_ctx))

    out = lax.map(one_row, (qg.transpose(1, 0, 2, 3, 4), row_start, dr_idx))
    return out.transpose(1, 0, 2, 3, 4).reshape(b, s, h * dh)


def context_attention(q, k, v):
    b, n, h, dh = q.shape
    s = jnp.einsum('bqhd,bkhd->bhqk', q, k).astype(jnp.float32) * dh ** -0.5
    p = jax.nn.softmax(s, axis=-1).astype(v.dtype)
    return jnp.einsum('bhqk,bkhd->bqhd', p, v).reshape(b, n, h * dh)


def gated_merge(h, y_four, y_attn, w_gate, w_fo, w_ao, w_o):
    gates = jax.nn.sigmoid(h @ w_gate)
    g_f, g_a = jnp.split(gates, N_BRANCHES, axis=-1)
    merged = g_f * (y_four @ w_fo) + g_a * (y_attn @ w_ao)
    return merged @ w_o


def sqrelu_mlp(h, w1, w2):
    a = jax.nn.relu(h @ w1)
    return (a * a) @ w2


def setup_inputs(seed: int = 0) -> dict:
    key = jax.random.key(seed)
    ks = jax.random.split(key, 20)
    nrm = lambda k, shape, s: jax.random.normal(k, shape, jnp.float32) * s
    return {
        "x": nrm(ks[0], (BATCH, SEQ, D_MODEL), 1.0),
        "c": nrm(ks[1], (BATCH, D_MODEL), 1.0),
        "ctx": nrm(ks[2], (BATCH, CTX_LEN, D_MODEL), 1.0),
        "c_ctx": nrm(ks[3], (D_MODEL,), 1.0),
        "w_ada": nrm(ks[4], (DEPTH, D_MODEL, N_MOD * D_MODEL), D_MODEL ** -0.5),
        "b_ada": nrm(ks[5], (DEPTH, N_MOD * D_MODEL), 0.02),
        "norm1_g": 1.0 + nrm(ks[6], (DEPTH, D_MODEL), 0.02),
        "norm2_g": 1.0 + nrm(ks[7], (DEPTH, D_MODEL), 0.02),
        "w_in": nrm(ks[8], (DEPTH, D_MODEL, IN_WIDTH), D_MODEL ** -0.5),
        "q_norm_g": 1.0 + nrm(ks[9], (DEPTH, HEAD_DIM), 0.02),
        "k_norm_g": 1.0 + nrm(ks[10], (DEPTH, HEAD_DIM), 0.02),
        "rpb": nrm(ks[11], (DEPTH, NA_HEADS, 2 * WIN_ROWS - 1, 2 * WIN_COLS - 1), 0.1),
        "w_branch_gate": nrm(ks[12], (DEPTH, D_MODEL, N_BRANCHES * D_MODEL), D_MODEL ** -0.5),
        "w_fourier_out": nrm(ks[13], (DEPTH, FOURIER_WIDTH, D_MODEL), FOURIER_WIDTH ** -0.5),
        "w_attn_out": nrm(ks[14], (DEPTH, NA_WIDTH, D_MODEL), NA_WIDTH ** -0.5),
        "w_out": nrm(ks[15], (DEPTH, D_MODEL, D_MODEL), D_MODEL ** -0.5),
        "w_mlp1": nrm(ks[16], (DEPTH, D_MODEL, D_FF), D_MODEL ** -0.5),
        "w_mlp2": nrm(ks[17], (DEPTH, D_FF, D_MODEL), D_FF ** -0.5),
    }


def reference(x, c, ctx, c_ctx, w_ada, b_ada, norm1_g, norm2_g, w_in, q_norm_g, k_norm_g,
              rpb, w_branch_gate, w_fourier_out, w_attn_out, w_out, w_mlp1, w_mlp2):
    h_ctx = ctx
    for l in range(DEPTH):
        last = l == DEPTH - 1
        sh1, sc1, g1, sh2, sc2, g2 = [m[:, None, :] for m in ada_mod(c, w_ada[l], b_ada[l])]
        csh1, csc1, cg1, csh2, csc2, cg2 = ada_mod(c_ctx, w_ada[l], b_ada[l])

        hc = modulate(rms_norm(h_ctx, norm1_g[l]), csh1, csc1)
        fc, qc, kc, vc = split_projection(hc @ w_in[l])
        kc = rms_norm(kc, k_norm_g[l])

        hx = modulate(rms_norm(x, norm1_g[l]), sh1, sc1)
        fx, qx, kx, vx = split_projection(hx @ w_in[l])
        qx = axial_rope(rms_norm(qx, q_norm_g[l]))
        kx = axial_rope(rms_norm(kx, k_norm_g[l]))
        y_four = fourier_mix(fx)
        y_attn = neighbourhood_attention(qx, kx, vx, kc, vc, rpb[l])
        mixed = gated_merge(hx, y_four, y_attn, w_branch_gate[l], w_fourier_out[l],
                            w_attn_out[l], w_out[l])
        x = x + g1 * mixed
        hx2 = modulate(rms_norm(x, norm2_g[l]), sh2, sc2)
        x = x + g2 * sqrelu_mlp(hx2, w_mlp1[l], w_mlp2[l])

        if not last:
            qc = rms_norm(qc, q_norm_g[l])
            yc_four = fourier_mix(fc)
            yc_attn = context_attention(qc, kc, vc)
            mixed_c = gated_merge(hc, yc_four, yc_attn, w_branch_gate[l], w_fourier_out[l],
                                  w_attn_out[l], w_out[l])
            h_ctx = h_ctx + cg1 * mixed_c
            hc2 = modulate(rms_norm(h_ctx, norm2_g[l]), csh2, csc2)
            h_ctx = h_ctx + cg2 * sqrelu_mlp(hc2, w_mlp1[l], w_mlp2[l])
    return x
```

```python
import functools

import numpy as np
import jax
import jax.numpy as jnp
from jax import lax
from jax.experimental import pallas as pl
from jax.experimental.pallas import tpu as pltpu

GRID_W = 64
N_FOURIER_GROUPS = 4
FOURIER_GROUP_DIM = 128
FOURIER_WIDTH = N_FOURIER_GROUPS * FOURIER_GROUP_DIM
NA_HEADS = 8
HEAD_DIM = 64
NA_WIDTH = NA_HEADS * HEAD_DIM
WIN_ROWS = 8
WIN_COLS = 16
ROPE_THETA = 10000.0
NORM_EPS = 1e-6
N_MOD = 6
MASK_VALUE = -1e30

LANES = 128
VMEM_LIMIT_BYTES = 48 * 1024 * 1024

BF16 = jnp.bfloat16
F32 = jnp.float32


def _dot(a, b):
    return jnp.dot(a, b, preferred_element_type=F32)


def _dot_nt(a, b):
    return lax.dot_general(a, b, (((1,), (1,)), ((), ())), preferred_element_type=F32)


def _params(*semantics):
    return pltpu.CompilerParams(dimension_semantics=semantics,
                                vmem_limit_bytes=VMEM_LIMIT_BYTES)


def _rms_mod(x, g, shift, scale):
    ms = jnp.mean(x * x, axis=-1, keepdims=True)
    return (x * lax.rsqrt(ms + NORM_EPS) * g) * (1.0 + scale) + shift


def _head_norm(a, gain, head_mean):
    half = NA_WIDTH // 2
    sq = (a * a).astype(BF16)
    ms = jnp.concatenate([_dot(sq[:, :half], head_mean), _dot(sq[:, half:], head_mean)], axis=1)
    return a * lax.rsqrt(ms + NORM_EPS) * gain


def _ada_kernel(cond_ref, w_ref, b_ref, o_ref):
    cond = cond_ref[...]
    s = cond * jax.nn.sigmoid(cond)
    s_hi = s.astype(BF16)
    s_lo = (s - s_hi.astype(F32)).astype(BF16)
    w = w_ref[...]
    w_hi = w.astype(BF16)
    w_lo = (w - w_hi.astype(F32)).astype(BF16)
    o_ref[...] = _dot(s_hi, w_hi) + _dot(s_hi, w_lo) + _dot(s_lo, w_hi) + b_ref[...]


def _ada(cond, w, b):
    rows, d = cond.shape
    n = w.shape[1]
    tn = 768
    return pl.pallas_call(
        _ada_kernel,
        grid=(n // tn,),
        in_specs=[pl.BlockSpec((rows, d), lambda j: (0, 0)),
                  pl.BlockSpec((d, tn), lambda j: (0, j)),
                  pl.BlockSpec((1, tn), lambda j: (0, j))],
        out_specs=pl.BlockSpec((rows, tn), lambda j: (0, j)),
        out_shape=jax.ShapeDtypeStruct((rows, n), F32),
        compiler_params=_params("arbitrary"),
        name="ada",
    )(cond, w, b)


def _ctx_kv_kernel(ctx_ref, mod_ref, g1_ref, w_ref, kg_ref, hm_ref, k_ref, v_ref):
    m = mod_ref[...]
    h = _rms_mod(ctx_ref[0], g1_ref[...], m[0:1], m[1:2])
    p = _dot(h.astype(BF16), w_ref[...])
    k_ref[0] = _head_norm(p[:, :NA_WIDTH], kg_ref[...], hm_ref[...]).astype(BF16)
    v_ref[0] = p[:, NA_WIDTH:].astype(BF16)


def _ctx_kv(ctx, mod_ctx, g1, w_in, kg, head_mean):
    b, c, d = ctx.shape
    kv_cols = 2 * NA_WIDTH
    out = jax.ShapeDtypeStruct((b, c, NA_WIDTH), BF16)
    return pl.pallas_call(
        _ctx_kv_kernel,
        grid=(b,),
        in_specs=[pl.BlockSpec((1, c, d), lambda i: (i, 0, 0)),
                  pl.BlockSpec((8, d), lambda i: (0, 0)),
                  pl.BlockSpec((1, d), lambda i: (0, 0)),
                  pl.BlockSpec((d, kv_cols), lambda i: (0, (w_in.shape[1] - kv_cols) // kv_cols)),
                  pl.BlockSpec((1, NA_WIDTH), lambda i: (0, 0)),
                  pl.BlockSpec(head_mean.shape, lambda i: (0, 0))],
        out_specs=[pl.BlockSpec((1, c, NA_WIDTH), lambda i: (i, 0, 0))] * 2,
        out_shape=[out, out],
        compiler_params=_params("arbitrary"),
        name="ctx_kv",
    )(ctx, mod_ctx, g1, w_in, kg, head_mean)


def _rope(a, cos, sin_signed, first_half):
    width = a.shape[-1]
    half = HEAD_DIM // 4
    partner = jnp.where(first_half, pltpu.roll(a, width - half, 1), pltpu.roll(a, half, 1))
    return a * cos + partner * sin_signed


def _proj_kernel(x_ref, mod_ref, g1_ref, w_ref, qg_ref, kg_ref, hm_ref, cos_ref, sin_ref,
                 f_ref, q_ref, k_ref, v_ref):
    m = mod_ref[0]
    h = _rms_mod(x_ref[0], g1_ref[...], m[0:1], m[1:2])
    p = _dot(h.astype(BF16), w_ref[...])
    f_ref[0] = p[:, :FOURIER_WIDTH].astype(BF16)
    q = p[:, FOURIER_WIDTH:FOURIER_WIDTH + NA_WIDTH]
    k = p[:, FOURIER_WIDTH + NA_WIDTH:FOURIER_WIDTH + 2 * NA_WIDTH]
    v_ref[0] = p[:, FOURIER_WIDTH + 2 * NA_WIDTH:].astype(BF16)
    reps = NA_WIDTH // cos_ref.shape[1]
    cos = jnp.concatenate([cos_ref[...]] * reps, axis=1)
    sin = jnp.concatenate([sin_ref[...]] * reps, axis=1)
    lane = lax.broadcasted_iota(jnp.int32, q.shape, 1)
    first_half = (lane % (HEAD_DIM // 2)) < (HEAD_DIM // 4)
    hm = hm_ref[...]
    q_ref[0] = _rope(_head_norm(q, qg_ref[...], hm), cos, sin, first_half).astype(BF16)
    k_ref[0] = _rope(_head_norm(k, kg_ref[...], hm), cos, sin, first_half).astype(BF16)


def _proj(x, mod, g1, w_in, qg, kg, head_mean, cos, sin, tm):
    b, s, d = x.shape
    out = jax.ShapeDtypeStruct((b, s, NA_WIDTH), BF16)
    tok = lambda t, i: (i, t, 0)
    const = lambda t, i: (0, 0)
    return pl.pallas_call(
        _proj_kernel,
        grid=(s // tm, b),
        in_specs=[pl.BlockSpec((1, tm, d), tok),
                  pl.BlockSpec((1, 8, d), lambda t, i: (i, 0, 0)),
                  pl.BlockSpec((1, d), const),
                  pl.BlockSpec(w_in.shape, const),
                  pl.BlockSpec((1, NA_WIDTH), const),
                  pl.BlockSpec((1, NA_WIDTH), const),
                  pl.BlockSpec(head_mean.shape, const),
                  pl.BlockSpec((tm, cos.shape[1]), lambda t, i: (t, 0)),
                  pl.BlockSpec((tm, sin.shape[1]), lambda t, i: (t, 0))],
        out_specs=[pl.BlockSpec((1, tm, NA_WIDTH), tok)] * 4,
        out_shape=[out] * 4,
        compiler_params=_params("arbitrary", "arbitrary"),
        name="proj",
    )(x, mod, g1, w_in, qg, kg, head_mean, cos, sin)


def _fourier1_kernel(u_ref, cs_ref, w1_ref, t_ref, z_ref):
    rows = u_ref.shape[1]
    gd = FOURIER_GROUP_DIM
    for j in range(u_ref.shape[2] // gd):
        zc = _dot(u_ref[0, :, j * gd:(j + 1) * gd], cs_ref[...])
        z_ref[0:rows, j * gd:(j + 1) * gd] = zc[:, :gd].astype(BF16)
        z_ref[rows:2 * rows, j * gd:(j + 1) * gd] = zc[:, gd:].astype(BF16)
    t = _dot(w1_ref[...], z_ref[...])
    t_ref[0, 0] = t[:rows].astype(BF16)
    t_ref[0, 1] = t[rows:].astype(BF16)


def _fourier1(f, cs, w1, bl):
    b, s, width = f.shape
    rows = s // GRID_W
    u = f.reshape(b, rows, GRID_W * width)
    cols = bl * width
    return pl.pallas_call(
        _fourier1_kernel,
        grid=(b, GRID_W // bl),
        in_specs=[pl.BlockSpec((1, rows, cols), lambda i, j: (i, 0, j)),
                  pl.BlockSpec(cs.shape, lambda i, j: (0, 0)),
                  pl.BlockSpec(w1.shape, lambda i, j: (0, 0))],
        out_specs=pl.BlockSpec((1, 2, rows, cols), lambda i, j: (i, 0, 0, j)),
        out_shape=jax.ShapeDtypeStruct((b, 2, rows, GRID_W * width), BF16),
        scratch_shapes=[pltpu.VMEM((2 * rows, cols), BF16)],
        compiler_params=_params("arbitrary", "arbitrary"),
        name="fourier1",
    )(u, cs, w1)


def _fourier2_kernel(t_ref, g_ref, y_ref):
    for j in range(t_ref.shape[2]):
        rhs = jnp.concatenate([t_ref[0, 0, j], t_ref[0, 1, j]], axis=0)
        y_ref[0, j] = _dot(g_ref[j], rhs).astype(BF16)


def _fourier2(t, g2, kab):
    b, _, rows, _ = t.shape
    width = FOURIER_WIDTH
    t5 = t.reshape(b, 2, rows, GRID_W, width)
    return pl.pallas_call(
        _fourier2_kernel,
        grid=(b, rows // kab),
        in_specs=[pl.BlockSpec((1, 2, kab, GRID_W, width), lambda i, j: (i, 0, j, 0, 0)),
                  pl.BlockSpec((kab, GRID_W, 2 * GRID_W), lambda i, j: (j, 0, 0))],
        out_specs=pl.BlockSpec((1, kab, GRID_W, width), lambda i, j: (i, j, 0, 0)),
        out_shape=jax.ShapeDtypeStruct((b, rows, GRID_W, width), BF16),
        compiler_params=_params("arbitrary", "arbitrary"),
        name="fourier2",
    )(t5, g2)


def _attn_kernel(q_ref, k_ref, v_ref, kc_ref, vc_ref, bias_ref, o_ref):
    r = pl.program_id(1)
    n_rows = k_ref.shape[1] // GRID_W
    row_start = jnp.clip(r - WIN_ROWS // 2, 0, n_rows - WIN_ROWS)
    start = pl.multiple_of(row_start * GRID_W, GRID_W)
    n_loc = WIN_ROWS * GRID_W
    lane = lax.broadcasted_iota(jnp.int32, (GRID_W, LANES), 1)
    low = lane < HEAD_DIM
    for pair in range(NA_WIDTH // LANES):
        sl = slice(pair * LANES, (pair + 1) * LANES)
        qp = q_ref[0, :, sl]
        zero = jnp.zeros_like(qp)
        q2 = jnp.concatenate([jnp.where(low, qp, zero), jnp.where(low, zero, qp)], axis=0)
        kw = k_ref[0, pl.ds(start, n_loc), sl]
        vw = v_ref[0, pl.ds(start, n_loc), sl]
        s_loc = _dot_nt(q2, kw) + bias_ref[0, pair]
        s_ctx = _dot_nt(q2, kc_ref[0, :, sl])
        m = jnp.maximum(jnp.max(s_loc, axis=-1, keepdims=True),
                        jnp.max(s_ctx, axis=-1, keepdims=True))
        p_loc = jnp.exp(s_loc - m)
        p_ctx = jnp.exp(s_ctx - m)
        denom = jnp.sum(p_loc, axis=-1, keepdims=True) + jnp.sum(p_ctx, axis=-1, keepdims=True)
        o = _dot(p_loc.astype(BF16), vw) + _dot(p_ctx.astype(BF16), vc_ref[0, :, sl])
        o = o / denom
        o_ref[0, :, sl] = jnp.where(low, o[:GRID_W], o[GRID_W:]).astype(BF16)


def _attn(q, k, v, kc, vc, bias):
    b, s, width = q.shape
    n_rows = s // GRID_W
    c = kc.shape[1]

    def bias_idx(i, r):
        return (r - jnp.clip(r - WIN_ROWS // 2, 0, n_rows - WIN_ROWS), 0, 0, 0)

    return pl.pallas_call(
        _attn_kernel,
        grid=(b, n_rows),
        in_specs=[pl.BlockSpec((1, GRID_W, width), lambda i, r: (i, r, 0)),
                  pl.BlockSpec((1, s, width), lambda i, r: (i, 0, 0)),
                  pl.BlockSpec((1, s, width), lambda i, r: (i, 0, 0)),
                  pl.BlockSpec((1, c, width), lambda i, r: (i, 0, 0)),
                  pl.BlockSpec((1, c, width), lambda i, r: (i, 0, 0)),
                  pl.BlockSpec((1,) + bias.shape[1:], bias_idx)],
        out_specs=pl.BlockSpec((1, GRID_W, width), lambda i, r: (i, r, 0)),
        out_shape=jax.ShapeDtypeStruct((b, s, width), BF16),
        compiler_params=_params("arbitrary", "arbitrary"),
        name="attn",
    )(q, k, v, kc, vc, bias)


def _merge_kernel(x_ref, ya_ref, yf_ref, mod_ref, g1_ref, wg_ref, wfo_ref, wao_ref, wo_ref,
                  o_ref):
    kab = yf_ref.shape[1]
    d = g1_ref.shape[1]
    x = jnp.concatenate([x_ref[0, :, j * d:(j + 1) * d] for j in range(kab)], axis=0)
    ya = jnp.concatenate([ya_ref[0, :, j * NA_WIDTH:(j + 1) * NA_WIDTH] for j in range(kab)],
                         axis=0)
    yf = yf_ref[0].reshape(kab * GRID_W, FOURIER_WIDTH)
    m = mod_ref[0]
    h = _rms_mod(x, g1_ref[...], m[0:1], m[1:2]).astype(BF16)
    gates = jax.nn.sigmoid(_dot(h, wg_ref[...]))
    merged = gates[:, :d] * _dot(yf, wfo_ref[...]) + gates[:, d:] * _dot(ya, wao_ref[...])
    x1 = x + m[2:3] * _dot(merged.astype(BF16), wo_ref[...])
    for j in range(kab):
        o_ref[0, :, j * d:(j + 1) * d] = x1[j * GRID_W:(j + 1) * GRID_W]


def _merge(x, y_attn, y_four, mod, g1, w_gate, w_fo, w_ao, w_o, kab):
    b, s, d = x.shape
    rows = s // GRID_W
    xv = x.reshape(b, GRID_W, rows * d)
    yav = y_attn.reshape(b, GRID_W, rows * NA_WIDTH)
    const = lambda i, j: (0, 0)
    out = pl.pallas_call(
        _merge_kernel,
        grid=(b, rows // kab),
        in_specs=[pl.BlockSpec((1, GRID_W, kab * d), lambda i, j: (i, 0, j)),
                  pl.BlockSpec((1, GRID_W, kab * NA_WIDTH), lambda i, j: (i, 0, j)),
                  pl.BlockSpec((1, kab, GRID_W, FOURIER_WIDTH), lambda i, j: (i, j, 0, 0)),
                  pl.BlockSpec((1, 8, d), lambda i, j: (i, 0, 0)),
                  pl.BlockSpec((1, d), const),
                  pl.BlockSpec(w_gate.shape, const),
                  pl.BlockSpec(w_fo.shape, const),
                  pl.BlockSpec(w_ao.shape, const),
                  pl.BlockSpec(w_o.shape, const)],
        out_specs=pl.BlockSpec((1, GRID_W, kab * d), lambda i, j: (i, 0, j)),
        out_shape=jax.ShapeDtypeStruct(xv.shape, F32),
        compiler_params=_params("arbitrary", "arbitrary"),
        name="merge",
    )(xv, yav, y_four, mod, g1, w_gate, w_fo, w_ao, w_o)
    return out.reshape(b, s, d)


def _mlp_kernel(x_ref, mod_ref, g2_ref, w1_ref, w2_ref, o_ref, *, ff_chunk):
    x = x_ref[0]
    m = mod_ref[0]
    h = _rms_mod(x, g2_ref[...], m[3:4], m[4:5]).astype(BF16)
    acc = jnp.zeros(x.shape, F32)
    for c in range(w1_ref.shape[1] // ff_chunk):
        a = jnp.maximum(_dot(h, w1_ref[:, c * ff_chunk:(c + 1) * ff_chunk]), 0.0)
        acc = acc + _dot((a * a).astype(BF16), w2_ref[c * ff_chunk:(c + 1) * ff_chunk, :])
    o_ref[0] = x + m[5:6] * acc


def _mlp(x, mod, g2, w1, w2, tm, ff_chunk):
    b, s, d = x.shape
    const = lambda i, t: (0, 0)
    return pl.pallas_call(
        functools.partial(_mlp_kernel, ff_chunk=ff_chunk),
        grid=(b, s // tm),
        in_specs=[pl.BlockSpec((1, tm, d), lambda i, t: (i, t, 0)),
                  pl.BlockSpec((1, 8, d), lambda i, t: (i, 0, 0)),
                  pl.BlockSpec((1, d), const),
                  pl.BlockSpec(w1.shape, const),
                  pl.BlockSpec(w2.shape, const)],
        out_specs=pl.BlockSpec((1, tm, d), lambda i, t: (i, t, 0)),
        out_shape=jax.ShapeDtypeStruct(x.shape, F32),
        compiler_params=_params("arbitrary", "arbitrary"),
        name="mlp",
    )(x, mod, g2, w1, w2)


def _dft_tables(rows):
    n = rows * GRID_W
    gd = FOURIER_GROUP_DIM
    c = np.arange(gd)
    ang = 2.0 * np.pi * np.outer(c, c) / gd
    cs = np.concatenate([np.cos(ang), np.sin(ang)], axis=1) * 2.0 ** -3
    a = np.arange(rows)
    ang = 2.0 * np.pi * np.outer(a, a) / rows
    c1, s1 = np.cos(ang), np.sin(ang)
    w1 = np.block([[c1, -s1], [s1, c1]]) * 2.0 ** -4
    b = np.arange(GRID_W)
    k = a[:, None] + rows * b[None, :]
    ang = 2.0 * np.pi * k[:, :, None] * b[None, None, :] / n
    g2 = np.concatenate([np.cos(ang), -np.sin(ang)], axis=2) * 2.0 ** -3
    assert gd * rows * GRID_W == 2 ** 20
    return tuple(jnp.asarray(m, F32).astype(BF16) for m in (cs, w1, g2))


def _rope_tables(seq):
    t = np.arange(seq)
    axis_dim = HEAD_DIM // 2
    inv_freq = ROPE_THETA ** (-np.arange(0, axis_dim, 2, dtype=np.float64) / axis_dim)
    ang_r = (t // GRID_W)[:, None] * inv_freq
    ang_c = (t % GRID_W)[:, None] * inv_freq
    ang = np.concatenate([ang_r, ang_r, ang_c, ang_c], axis=1)
    sign = np.tile(np.repeat([-1.0, 1.0], axis_dim // 2), 2)
    reps = LANES // HEAD_DIM
    cos = np.tile(np.cos(ang), (1, reps))
    sin = np.tile(np.sin(ang) * sign, (1, reps))
    return jnp.asarray(cos, F32), jnp.asarray(sin, F32)


def _bias_table(rpb, n_rows):
    del n_rows
    var = np.arange(WIN_ROWS)
    j = np.arange(WIN_ROWS)
    dr = j[None, :] - var[:, None] + (WIN_ROWS - 1)
    c = np.arange(GRID_W)
    col_start = np.clip(c - WIN_COLS // 2, 0, GRID_W - WIN_COLS)
    kcol = np.arange(GRID_W)
    valid = (kcol[None, :] >= col_start[:, None]) & (kcol[None, :] < col_start[:, None] + WIN_COLS)
    dc = np.clip(kcol[None, :] - c[:, None] + (WIN_COLS - 1), 0, 2 * WIN_COLS - 2)
    tab = rpb[:, dr[:, None, :, None], dc[None, :, None, :]]
    tab = jnp.where(valid[None, None, :, None, :], tab, MASK_VALUE)
    tab = tab.transpose(1, 0, 2, 3, 4)
    heads_per_block = LANES // HEAD_DIM
    return tab.reshape(WIN_ROWS, NA_HEADS // heads_per_block, heads_per_block * GRID_W,
                       WIN_ROWS * GRID_W).astype(F32)


def kernel(x, c, ctx, c_ctx, w_ada, b_ada, norm1_g, norm2_g, w_in, q_norm_g, k_norm_g, rpb,
           w_branch_gate, w_fourier_out, w_attn_out, w_out, w_mlp1, w_mlp2):
    b, s, d = x.shape
    assert w_ada.shape[0] == 1, "single-layer block only"
    n_rows = s // GRID_W
    assert n_rows >= WIN_ROWS and b < 8

    cond = jnp.zeros((8, d), F32).at[:b].set(c).at[b].set(c_ctx)
    mods = _ada(cond, w_ada[0], b_ada)
    mod = jnp.pad(mods[:b].reshape(b, N_MOD, d), ((0, 0), (0, 8 - N_MOD), (0, 0)))
    mod_ctx = jnp.pad(mods[b].reshape(N_MOD, d), ((0, 8 - N_MOD), (0, 0)))

    half = NA_WIDTH // 2
    head_mean = jnp.asarray(
        np.kron(np.eye(half // HEAD_DIM), np.full((HEAD_DIM, HEAD_DIM), 1.0 / HEAD_DIM)),
        F32).astype(BF16)
    qg = jnp.tile(q_norm_g[0], NA_HEADS)[None, :] * HEAD_DIM ** -0.5
    kg = jnp.tile(k_norm_g[0], NA_HEADS)[None, :]
    cos, sin = _rope_tables(s)
    cs, w1, g2 = _dft_tables(n_rows)
    w_in_b = w_in[0].astype(BF16)

    kc, vc = _ctx_kv(ctx, mod_ctx, norm1_g, w_in_b, kg, head_mean)
    f, q, k, v = _proj(x, mod, norm1_g, w_in_b, qg, kg, head_mean, cos, sin, tm=512)
    t = _fourier1(f, cs, w1, bl=8)
    y_four = _fourier2(t, g2, kab=8)
    y_attn = _attn(q, k, v, kc, vc, _bias_table(rpb[0], n_rows))
    x1 = _merge(x, y_attn, y_four, mod, norm1_g, w_branch_gate[0].astype(BF16),
                w_fourier_out[0].astype(BF16), w_attn_out[0].astype(BF16),
                w_out[0].astype(BF16), kab=8)
    return _mlp(x1, mod, norm2_g, w_mlp1[0].astype(BF16), w_mlp2[0].astype(BF16),
                tm=512, ff_chunk=1024)
```

```python
import functools

import numpy as np
import jax
import jax.numpy as jnp
from jax import lax
from jax.experimental import pallas as pl
from jax.experimental.pallas import tpu as pltpu

GRID_W = 64
N_FOURIER_GROUPS = 4
FOURIER_GROUP_DIM = 128
FOURIER_WIDTH = N_FOURIER_GROUPS * FOURIER_GROUP_DIM
NA_HEADS = 8
HEAD_DIM = 64
NA_WIDTH = NA_HEADS * HEAD_DIM
WIN_ROWS = 8
WIN_COLS = 16
ROPE_THETA = 10000.0
NORM_EPS = 1e-6
N_MOD = 6
MASK_VALUE = -1e30

LANES = 128
VMEM_LIMIT_BYTES = 48 * 1024 * 1024

BF16 = jnp.bfloat16
F32 = jnp.float32


def _dot(a, b):
    return jnp.dot(a, b, preferred_element_type=F32)


def _dot_nt(a, b):
    return lax.dot_general(a, b, (((1,), (1,)), ((), ())), preferred_element_type=F32)


def _params(*semantics):
    return pltpu.CompilerParams(dimension_semantics=semantics,
                                vmem_limit_bytes=VMEM_LIMIT_BYTES)


def _rms_mod(x, g, shift, scale):
    ms = jnp.mean(x * x, axis=-1, keepdims=True)
    return (x * lax.rsqrt(ms + NORM_EPS) * g) * (1.0 + scale) + shift


def _head_norm(a, gain, head_mean):
    half = NA_WIDTH // 2
    sq = (a * a).astype(BF16)
    ms = jnp.concatenate([_dot(sq[:, :half], head_mean), _dot(sq[:, half:], head_mean)], axis=1)
    return a * lax.rsqrt(ms + NORM_EPS) * gain


def _ada_kernel(cond_ref, w_ref, b_ref, o_ref):
    cond = cond_ref[...]
    s = cond * jax.nn.sigmoid(cond)
    s_hi = s.astype(BF16)
    s_lo = (s - s_hi.astype(F32)).astype(BF16)
    w = w_ref[...]
    w_hi = w.astype(BF16)
    w_lo = (w - w_hi.astype(F32)).astype(BF16)
    o_ref[...] = _dot(s_hi, w_hi) + _dot(s_hi, w_lo) + _dot(s_lo, w_hi) + b_ref[...]


def _ada(cond, w, b):
    rows, d = cond.shape
    n = w.shape[1]
    tn = 768
    return pl.pallas_call(
        _ada_kernel,
        grid=(n // tn,),
        in_specs=[pl.BlockSpec((rows, d), lambda j: (0, 0)),
                  pl.BlockSpec((d, tn), lambda j: (0, j)),
                  pl.BlockSpec((1, tn), lambda j: (0, j))],
        out_specs=pl.BlockSpec((rows, tn), lambda j: (0, j)),
        out_shape=jax.ShapeDtypeStruct((rows, n), F32),
        compiler_params=_params("arbitrary"),
        name="ada",
    )(cond, w, b)


def _ctx_kv_kernel(ctx_ref, mod_ref, g1_ref, w_ref, kg_ref, hm_ref, k_ref, v_ref):
    m = mod_ref[...]
    h = _rms_mod(ctx_ref[0], g1_ref[...], m[0:1], m[1:2])
    p = _dot(h.astype(BF16), w_ref[...])
    k_ref[0] = _head_norm(p[:, :NA_WIDTH], kg_ref[...], hm_ref[...]).astype(BF16)
    v_ref[0] = p[:, NA_WIDTH:].astype(BF16)


def _ctx_kv(ctx, mod_ctx, g1, w_in, kg, head_mean):
    b, c, d = ctx.shape
    kv_cols = 2 * NA_WIDTH
    out = jax.ShapeDtypeStruct((b, c, NA_WIDTH), BF16)
    return pl.pallas_call(
        _ctx_kv_kernel,
        grid=(b,),
        in_specs=[pl.BlockSpec((1, c, d), lambda i: (i, 0, 0)),
                  pl.BlockSpec((8, d), lambda i: (0, 0)),
                  pl.BlockSpec((1, d), lambda i: (0, 0)),
                  pl.BlockSpec((d, kv_cols), lambda i: (0, (w_in.shape[1] - kv_cols) // kv_cols)),
                  pl.BlockSpec((1, NA_WIDTH), lambda i: (0, 0)),
                  pl.BlockSpec(head_mean.shape, lambda i: (0, 0))],
        out_specs=[pl.BlockSpec((1, c, NA_WIDTH), lambda i: (i, 0, 0))] * 2,
        out_shape=[out, out],
        compiler_params=_params("arbitrary"),
        name="ctx_kv",
    )(ctx, mod_ctx, g1, w_in, kg, head_mean)


def _rope(a, cos, sin_signed, first_half):
    width = a.shape[-1]
    half = HEAD_DIM // 4
    partner = jnp.where(first_half, pltpu.roll(a, width - half, 1), pltpu.roll(a, half, 1))
    return a * cos + partner * sin_signed


def _proj_kernel(x_ref, mod_ref, g1_ref, w_ref, qg_ref, kg_ref, hm_ref, cos_ref, sin_ref,
                 f_ref, q_ref, k_ref, v_ref):
    m = mod_ref[0]
    h = _rms_mod(x_ref[0], g1_ref[...], m[0:1], m[1:2])
    p = _dot(h.astype(BF16), w_ref[...])
    f_ref[0] = p[:, :FOURIER_WIDTH].astype(BF16)
    q = p[:, FOURIER_WIDTH:FOURIER_WIDTH + NA_WIDTH]
    k = p[:, FOURIER_WIDTH + NA_WIDTH:FOURIER_WIDTH + 2 * NA_WIDTH]
    v_ref[0] = p[:, FOURIER_WIDTH + 2 * NA_WIDTH:].astype(BF16)
    reps = NA_WIDTH // cos_ref.shape[1]
    cos = jnp.concatenate([cos_ref[...]] * reps, axis=1)
    sin = jnp.concatenate([sin_ref[...]] * reps, axis=1)
    lane = lax.broadcasted_iota(jnp.int32, q.shape, 1)
    first_half = (lane % (HEAD_DIM // 2)) < (HEAD_DIM // 4)
    hm = hm_ref[...]
    q_ref[0] = _rope(_head_norm(q, qg_ref[...], hm), cos, sin, first_half).astype(BF16)
    k_ref[0] = _rope(_head_norm(k, kg_ref[...], hm), cos, sin, first_half).astype(BF16)


def _proj(x, mod, g1, w_in, qg, kg, head_mean, cos, sin, tm):
    b, s, d = x.shape
    out = jax.ShapeDtypeStruct((b, s, NA_WIDTH), BF16)
    tok = lambda t, i: (i, t, 0)
    const = lambda t, i: (0, 0)
    return pl.pallas_call(
        _proj_kernel,
        grid=(s // tm, b),
        in_specs=[pl.BlockSpec((1, tm, d), tok),
                  pl.BlockSpec((1, 8, d), lambda t, i: (i, 0, 0)),
                  pl.BlockSpec((1, d), const),
                  pl.BlockSpec(w_in.shape, const),
                  pl.BlockSpec((1, NA_WIDTH), const),
                  pl.BlockSpec((1, NA_WIDTH), const),
                  pl.BlockSpec(head_mean.shape, const),
                  pl.BlockSpec((tm, cos.shape[1]), lambda t, i: (t, 0)),
                  pl.BlockSpec((tm, sin.shape[1]), lambda t, i: (t, 0))],
        out_specs=[pl.BlockSpec((1, tm, NA_WIDTH), tok)] * 4,
        out_shape=[out] * 4,
        compiler_params=_params("arbitrary", "arbitrary"),
        name="proj",
    )(x, mod, g1, w_in, qg, kg, head_mean, cos, sin)


def _fourier1_kernel(u_ref, cs_ref, w1_ref, t_ref, z_ref):
    rows = u_ref.shape[1]
    gd = FOURIER_GROUP_DIM
    for j in range(u_ref.shape[2] // gd):
        zc = _dot(u_ref[0, :, j * gd:(j + 1) * gd], cs_ref[...])
        z_ref[0:rows, j * gd:(j + 1) * gd] = zc[:, :gd].astype(BF16)
        z_ref[rows:2 * rows, j * gd:(j + 1) * gd] = zc[:, gd:].astype(BF16)
    t = _dot(w1_ref[...], z_ref[...])
    t_ref[0, 0] = t[:rows].astype(BF16)
    t_ref[0, 1] = t[rows:].astype(BF16)


def _fourier1(f, cs, w1, bl):
    b, s, width = f.shape
    rows = s // GRID_W
    u = f.reshape(b, rows, GRID_W * width)
    cols = bl * width
    return pl.pallas_call(
        _fourier1_kernel,
        grid=(b, GRID_W // bl),
        in_specs=[pl.BlockSpec((1, rows, cols), lambda i, j: (i, 0, j)),
                  pl.BlockSpec(cs.shape, lambda i, j: (0, 0)),
                  pl.BlockSpec(w1.shape, lambda i, j: (0, 0))],
        out_specs=pl.BlockSpec((1, 2, rows, cols), lambda i, j: (i, 0, 0, j)),
        out_shape=jax.ShapeDtypeStruct((b, 2, rows, GRID_W * width), BF16),
        scratch_shapes=[pltpu.VMEM((2 * rows, cols), BF16)],
        compiler_params=_params("arbitrary", "arbitrary"),
        name="fourier1",
    )(u, cs, w1)


def _fourier2_kernel(t_ref, g_ref, y_ref):
    for j in range(t_ref.shape[2]):
        rhs = jnp.concatenate([t_ref[0, 0, j], t_ref[0, 1, j]], axis=0)
        y_ref[0, j] = _dot(g_ref[j], rhs).astype(BF16)


def _fourier2(t, g2, kab):
    b, _, rows, _ = t.shape
    width = FOURIER_WIDTH
    t5 = t.reshape(b, 2, rows, GRID_W, width)
    return pl.pallas_call(
        _fourier2_kernel,
        grid=(b, rows // kab),
        in_specs=[pl.BlockSpec((1, 2, kab, GRID_W, width), lambda i, j: (i, 0, j, 0, 0)),
                  pl.BlockSpec((kab, GRID_W, 2 * GRID_W), lambda i, j: (j, 0, 0))],
        out_specs=pl.BlockSpec((1, kab, GRID_W, width), lambda i, j: (i, j, 0, 0)),
        out_shape=jax.ShapeDtypeStruct((b, rows, GRID_W, width), BF16),
        compiler_params=_params("arbitrary", "arbitrary"),
        name="fourier2",
    )(t5, g2)


def _attn_kernel(q_ref, k_ref, v_ref, kc_ref, vc_ref, bias_ref, o_ref):
    r = pl.program_id(1)
    n_rows = k_ref.shape[1] // GRID_W
    row_start = jnp.clip(r - WIN_ROWS // 2, 0, n_rows - WIN_ROWS)
    start = pl.multiple_of(row_start * GRID_W, GRID_W)
    n_loc = WIN_ROWS * GRID_W
    lane = lax.broadcasted_iota(jnp.int32, (GRID_W, LANES), 1)
    low = lane < HEAD_DIM
    for pair in range(NA_WIDTH // LANES):
        sl = slice(pair * LANES, (pair + 1) * LANES)
        qp = q_ref[0, :, sl]
        zero = jnp.zeros_like(qp)
        q2 = jnp.concatenate([jnp.where(low, qp, zero), jnp.where(low, zero, qp)], axis=0)
        kw = k_ref[0, pl.ds(start, n_loc), sl]
        vw = v_ref[0, pl.ds(start, n_loc), sl]
        s_loc = _dot_nt(q2, kw) + bias_ref[0, pair]
        s_ctx = _dot_nt(q2, kc_ref[0, :, sl])
        m = jnp.maximum(jnp.max(s_loc, axis=-1, keepdims=True),
                        jnp.max(s_ctx, axis=-1, keepdims=True))
        p_loc = jnp.exp(s_loc - m)
        p_ctx = jnp.exp(s_ctx - m)
        denom = jnp.sum(p_loc, axis=-1, keepdims=True) + jnp.sum(p_ctx, axis=-1, keepdims=True)
        o = _dot(p_loc.astype(BF16), vw) + _dot(p_ctx.astype(BF16), vc_ref[0, :, sl])
        o = o / denom
        o_ref[0, :, sl] = jnp.where(low, o[:GRID_W], o[GRID_W:]).astype(BF16)


def _attn(q, k, v, kc, vc, bias):
    b, s, width = q.shape
    n_rows = s // GRID_W
    c = kc.shape[1]

    def bias_idx(i, r):
        return (r - jnp.clip(r - WIN_ROWS // 2, 0, n_rows - WIN_ROWS), 0, 0, 0)

    return pl.pallas_call(
        _attn_kernel,
        grid=(b, n_rows),
        in_specs=[pl.BlockSpec((1, GRID_W, width), lambda i, r: (i, r, 0)),
                  pl.BlockSpec((1, s, width), lambda i, r: (i, 0, 0)),
                  pl.BlockSpec((1, s, width), lambda i, r: (i, 0, 0)),
                  pl.BlockSpec((1, c, width), lambda i, r: (i, 0, 0)),
                  pl.BlockSpec((1, c, width), lambda i, r: (i, 0, 0)),
                  pl.BlockSpec((1,) + bias.shape[1:], bias_idx)],
        out_specs=pl.BlockSpec((1, GRID_W, width), lambda i, r: (i, r, 0)),
        out_shape=jax.ShapeDtypeStruct((b, s, width), BF16),
        compiler_params=_params("arbitrary", "arbitrary"),
        name="attn",
    )(q, k, v, kc, vc, bias)


def _merge_kernel(x_ref, ya_ref, yf_ref, mod_ref, g1_ref, wg_ref, wfo_ref, wao_ref, wo_ref,
                  o_ref):
    kab = yf_ref.shape[1]
    d = g1_ref.shape[1]
    x = jnp.concatenate([x_ref[0, :, j * d:(j + 1) * d] for j in range(kab)], axis=0)
    ya = jnp.concatenate([ya_ref[0, :, j * NA_WIDTH:(j + 1) * NA_WIDTH] for j in range(kab)],
                         axis=0)
    yf = yf_ref[0].reshape(kab * GRID_W, FOURIER_WIDTH)
    m = mod_ref[0]
    h = _rms_mod(x, g1_ref[...], m[0:1], m[1:2]).astype(BF16)
    gates = jax.nn.sigmoid(_dot(h, wg_ref[...]))
    merged = gates[:, :d] * _dot(yf, wfo_ref[...]) + gates[:, d:] * _dot(ya, wao_ref[...])
    x1 = x + m[2:3] * _dot(merged.astype(BF16), wo_ref[...])
    for j in range(kab):
        o_ref[0, :, j * d:(j + 1) * d] = x1[j * GRID_W:(j + 1) * GRID_W]


def _merge(x, y_attn, y_four, mod, g1, w_gate, w_fo, w_ao, w_o, kab):
    b, s, d = x.shape
    rows = s // GRID_W
    xv = x.reshape(b, GRID_W, rows * d)
    yav = y_attn.reshape(b, GRID_W, rows * NA_WIDTH)
    const = lambda i, j: (0, 0)
    out = pl.pallas_call(
        _merge_kernel,
        grid=(b, rows // kab),
        in_specs=[pl.BlockSpec((1, GRID_W, kab * d), lambda i, j: (i, 0, j)),
                  pl.BlockSpec((1, GRID_W, kab * NA_WIDTH), lambda i, j: (i, 0, j)),
                  pl.BlockSpec((1, kab, GRID_W, FOURIER_WIDTH), lambda i, j: (i, j, 0, 0)),
                  pl.BlockSpec((1, 8, d), lambda i, j: (i, 0, 0)),
                  pl.BlockSpec((1, d), const),
                  pl.BlockSpec(w_gate.shape, const),
                  pl.BlockSpec(w_fo.shape, const),
                  pl.BlockSpec(w_ao.shape, const),
                  pl.BlockSpec(w_o.shape, const)],
        out_specs=pl.BlockSpec((1, GRID_W, kab * d), lambda i, j: (i, 0, j)),
        out_shape=jax.ShapeDtypeStruct(xv.shape, F32),
        compiler_params=_params("arbitrary", "arbitrary"),
        name="merge",
    )(xv, yav, y_four, mod, g1, w_gate, w_fo, w_ao, w_o)
    return out.reshape(b, s, d)


def _mlp_kernel(x_ref, mod_ref, g2_ref, w1_ref, w2_ref, o_ref, *, ff_chunk):
    x = x_ref[0]
    m = mod_ref[0]
    h = _rms_mod(x, g2_ref[...], m[3:4], m[4:5]).astype(BF16)
    acc = jnp.zeros(x.shape, F32)
    for c in range(w1_ref.shape[1] // ff_chunk):
        a = jnp.maximum(_dot(h, w1_ref[:, c * ff_chunk:(c + 1) * ff_chunk]), 0.0)
        acc = acc + _dot((a * a).astype(BF16), w2_ref[c * ff_chunk:(c + 1) * ff_chunk, :])
    o_ref[0] = x + m[5:6] * acc


def _mlp(x, mod, g2, w1, w2, tm, ff_chunk):
    b, s, d = x.shape
    const = lambda i, t: (0, 0)
    return pl.pallas_call(
        functools.partial(_mlp_kernel, ff_chunk=ff_chunk),
        grid=(b, s // tm),
        in_specs=[pl.BlockSpec((1, tm, d), lambda i, t: (i, t, 0)),
                  pl.BlockSpec((1, 8, d), lambda i, t: (i, 0, 0)),
                  pl.BlockSpec((1, d), const),
                  pl.BlockSpec(w1.shape, const),
                  pl.BlockSpec(w2.shape, const)],
        out_specs=pl.BlockSpec((1, tm, d), lambda i, t: (i, t, 0)),
        out_shape=jax.ShapeDtypeStruct(x.shape, F32),
        compiler_params=_params("arbitrary", "arbitrary"),
        name="mlp",
    )(x, mod, g2, w1, w2)


def _dft_tables(rows):
    n = rows * GRID_W
    gd = FOURIER_GROUP_DIM
    c = np.arange(gd)
    ang = 2.0 * np.pi * np.outer(c, c) / gd
    cs = np.concatenate([np.cos(ang), np.sin(ang)], axis=1) * 2.0 ** -3
    a = np.arange(rows)
    ang = 2.0 * np.pi * np.outer(a, a) / rows
    c1, s1 = np.cos(ang), np.sin(ang)
    w1 = np.block([[c1, -s1], [s1, c1]]) * 2.0 ** -4
    b = np.arange(GRID_W)
    k = a[:, None] + rows * b[None, :]
    ang = 2.0 * np.pi * k[:, :, None] * b[None, None, :] / n
    g2 = np.concatenate([np.cos(ang), -np.sin(ang)], axis=2) * 2.0 ** -3
    assert gd * rows * GRID_W == 2 ** 20
    return tuple(jnp.asarray(m, F32).astype(BF16) for m in (cs, w1, g2))


def _rope_tables(seq):
    t = np.arange(seq)
    axis_dim = HEAD_DIM // 2
    inv_freq = ROPE_THETA ** (-np.arange(0, axis_dim, 2, dtype=np.float64) / axis_dim)
    ang_r = (t // GRID_W)[:, None] * inv_freq
    ang_c = (t % GRID_W)[:, None] * inv_freq
    ang = np.concatenate([ang_r, ang_r, ang_c, ang_c], axis=1)
    sign = np.tile(np.repeat([-1.0, 1.0], axis_dim // 2), 2)
    reps = LANES // HEAD_DIM
    cos = np.tile(np.cos(ang), (1, reps))
    sin = np.tile(np.sin(ang) * sign, (1, reps))
    return jnp.asarray(cos, F32), jnp.asarray(sin, F32)


def _bias_table(rpb, n_rows):
    del n_rows
    c = np.arange(GRID_W)
    col_start = np.clip(c - WIN_COLS // 2, 0, GRID_W - WIN_COLS)
    kcol = np.arange(GRID_W)
    valid = (kcol[None, :] >= col_start[:, None]) & (kcol[None, :] < col_start[:, None] + WIN_COLS)
    left = GRID_W - WIN_COLS
    padded = jnp.pad(rpb, ((0, 0), (0, 0), (left, GRID_W)))
    toep = jnp.stack([padded[:, :, GRID_W - 1 - q:2 * GRID_W - 1 - q] for q in range(GRID_W)],
                     axis=1)
    toep = jnp.where(valid[None, :, None, :], toep, MASK_VALUE)
    tab = jnp.stack([toep[:, :, WIN_ROWS - 1 - var:2 * WIN_ROWS - 1 - var]
                     for var in range(WIN_ROWS)], axis=0)
    heads_per_block = LANES // HEAD_DIM
    return tab.reshape(WIN_ROWS, NA_HEADS // heads_per_block, heads_per_block * GRID_W,
                       WIN_ROWS * GRID_W).astype(F32)


def kernel(x, c, ctx, c_ctx, w_ada, b_ada, norm1_g, norm2_g, w_in, q_norm_g, k_norm_g, rpb,
           w_branch_gate, w_fourier_out, w_attn_out, w_out, w_mlp1, w_mlp2):
    b, s, d = x.shape
    assert w_ada.shape[0] == 1, "single-layer block only"
    n_rows = s // GRID_W
    assert n_rows >= WIN_ROWS and b < 8

    cond = jnp.zeros((8, d), F32).at[:b].set(c).at[b].set(c_ctx)
    mods = _ada(cond, w_ada[0], b_ada)
    mod = jnp.pad(mods[:b].reshape(b, N_MOD, d), ((0, 0), (0, 8 - N_MOD), (0, 0)))
    mod_ctx = jnp.pad(mods[b].reshape(N_MOD, d), ((0, 8 - N_MOD), (0, 0)))

    half = NA_WIDTH // 2
    head_mean = jnp.asarray(
        np.kron(np.eye(half // HEAD_DIM), np.full((HEAD_DIM, HEAD_DIM), 1.0 / HEAD_DIM)),
        F32).astype(BF16)
    qg = jnp.tile(q_norm_g[0], NA_HEADS)[None, :] * HEAD_DIM ** -0.5
    kg = jnp.tile(k_norm_g[0], NA_HEADS)[None, :]
    cos, sin = _rope_tables(s)
    cs, w1, g2 = _dft_tables(n_rows)
    w_in_b = w_in[0].astype(BF16)

    kc, vc = _ctx_kv(ctx, mod_ctx, norm1_g, w_in_b, kg, head_mean)
    f, q, k, v = _proj(x, mod, norm1_g, w_in_b, qg, kg, head_mean, cos, sin, tm=512)
    t = _fourier1(f, cs, w1, bl=8)
    y_four = _fourier2(t, g2, kab=8)
    y_attn = _attn(q, k, v, kc, vc, _bias_table(rpb[0], n_rows))
    x1 = _merge(x, y_attn, y_four, mod, norm1_g, w_branch_gate[0].astype(BF16),
                w_fourier_out[0].astype(BF16), w_attn_out[0].astype(BF16),
                w_out[0].astype(BF16), kab=8)
    return _mlp(x1, mod, norm2_g, w_mlp1[0].astype(BF16), w_mlp2[0].astype(BF16),
                tm=512, ff_chunk=1024)
```

```python
import functools

import numpy as np
import jax
import jax.numpy as jnp
from jax import lax
from jax.experimental import pallas as pl
from jax.experimental.pallas import tpu as pltpu

GRID_W = 64
N_FOURIER_GROUPS = 4
FOURIER_GROUP_DIM = 128
FOURIER_WIDTH = N_FOURIER_GROUPS * FOURIER_GROUP_DIM
NA_HEADS = 8
HEAD_DIM = 64
NA_WIDTH = NA_HEADS * HEAD_DIM
WIN_ROWS = 8
WIN_COLS = 16
ROPE_THETA = 10000.0
NORM_EPS = 1e-6
N_MOD = 6
MASK_VALUE = -1e30
LOG2_E = 1.4426950408889634

LANES = 128
SUBLANES = 8
VMEM_LIMIT_BYTES = 48 * 1024 * 1024

BF16 = jnp.bfloat16
F32 = jnp.float32


def _dot(a, b):
    return jnp.dot(a, b, preferred_element_type=F32)


def _dot_nt(a, b):
    return lax.dot_general(a, b, (((1,), (1,)), ((), ())), preferred_element_type=F32)


def _params(*semantics):
    return pltpu.CompilerParams(dimension_semantics=semantics,
                                vmem_limit_bytes=VMEM_LIMIT_BYTES)


def _rms_mod(x, g, shift, scale):
    ms = jnp.mean(x * x, axis=-1, keepdims=True)
    return (x * lax.rsqrt(ms + NORM_EPS) * g) * (1.0 + scale) + shift


def _head_norm(a, gain, head_mean):
    half = NA_WIDTH // 2
    sq = (a * a).astype(BF16)
    ms = jnp.concatenate([_dot(sq[:, :half], head_mean), _dot(sq[:, half:], head_mean)], axis=1)
    return a * lax.rsqrt(ms + NORM_EPS) * gain


def _ada_kernel(cond_ref, w_ref, b_ref, o_ref):
    cond = cond_ref[...]
    s = cond * jax.nn.sigmoid(cond)
    s_hi = s.astype(BF16)
    s_lo = (s - s_hi.astype(F32)).astype(BF16)
    w = w_ref[...]
    w_hi = w.astype(BF16)
    w_lo = (w - w_hi.astype(F32)).astype(BF16)
    o_ref[...] = _dot(s_hi, w_hi) + _dot(s_hi, w_lo) + _dot(s_lo, w_hi) + b_ref[...]


def _ada(cond, w, b):
    rows, d = cond.shape
    n = w.shape[1]
    tn = 768
    return pl.pallas_call(
        _ada_kernel,
        grid=(n // tn,),
        in_specs=[pl.BlockSpec((rows, d), lambda j: (0, 0)),
                  pl.BlockSpec((d, tn), lambda j: (0, j)),
                  pl.BlockSpec((1, tn), lambda j: (0, j))],
        out_specs=pl.BlockSpec((rows, tn), lambda j: (0, j)),
        out_shape=jax.ShapeDtypeStruct((rows, n), F32),
        compiler_params=_params("arbitrary"),
        name="ada",
    )(cond, w, b)


def _ctx_kv_kernel(ctx_ref, mod_ref, g1_ref, w_ref, kg_ref, hm_ref, k_ref, v_ref):
    m = mod_ref[...]
    h = _rms_mod(ctx_ref[0], g1_ref[...], m[0:1], m[1:2])
    p = _dot(h.astype(BF16), w_ref[...])
    k_ref[0] = _head_norm(p[:, :NA_WIDTH], kg_ref[...], hm_ref[...]).astype(BF16)
    v_ref[0] = p[:, NA_WIDTH:].astype(BF16)


def _ctx_kv(ctx, mod_ctx, g1, w_in, kg, head_mean):
    b, c, d = ctx.shape
    kv_cols = 2 * NA_WIDTH
    out = jax.ShapeDtypeStruct((b, c, NA_WIDTH), BF16)
    return pl.pallas_call(
        _ctx_kv_kernel,
        grid=(b,),
        in_specs=[pl.BlockSpec((1, c, d), lambda i: (i, 0, 0)),
                  pl.BlockSpec((8, d), lambda i: (0, 0)),
                  pl.BlockSpec((1, d), lambda i: (0, 0)),
                  pl.BlockSpec((d, kv_cols), lambda i: (0, (w_in.shape[1] - kv_cols) // kv_cols)),
                  pl.BlockSpec((1, NA_WIDTH), lambda i: (0, 0)),
                  pl.BlockSpec(head_mean.shape, lambda i: (0, 0))],
        out_specs=[pl.BlockSpec((1, c, NA_WIDTH), lambda i: (i, 0, 0))] * 2,
        out_shape=[out, out],
        compiler_params=_params("arbitrary"),
        name="ctx_kv",
    )(ctx, mod_ctx, g1, w_in, kg, head_mean)


def _rope(a, cos, sin_signed, first_half):
    width = a.shape[-1]
    half = HEAD_DIM // 4
    partner = jnp.where(first_half, pltpu.roll(a, width - half, 1), pltpu.roll(a, half, 1))
    return a * cos + partner * sin_signed


def _proj_kernel(x_ref, mod_ref, g1_ref, w_ref, qg_ref, kg_ref, hm_ref, cos_ref, sin_ref,
                 f_ref, q_ref, k_ref, v_ref):
    m = mod_ref[0]
    h = _rms_mod(x_ref[0], g1_ref[...], m[0:1], m[1:2])
    p = _dot(h.astype(BF16), w_ref[...])
    f_ref[0] = p[:, :FOURIER_WIDTH].astype(BF16)
    q = p[:, FOURIER_WIDTH:FOURIER_WIDTH + NA_WIDTH]
    k = p[:, FOURIER_WIDTH + NA_WIDTH:FOURIER_WIDTH + 2 * NA_WIDTH]
    v_ref[0] = p[:, FOURIER_WIDTH + 2 * NA_WIDTH:].astype(BF16)
    reps = NA_WIDTH // cos_ref.shape[1]
    cos = jnp.concatenate([cos_ref[...]] * reps, axis=1)
    sin = jnp.concatenate([sin_ref[...]] * reps, axis=1)
    lane = lax.broadcasted_iota(jnp.int32, q.shape, 1)
    first_half = (lane % (HEAD_DIM // 2)) < (HEAD_DIM // 4)
    hm = hm_ref[...]
    q_ref[0] = _rope(_head_norm(q, qg_ref[...], hm), cos, sin, first_half).astype(BF16)
    k_ref[0] = _rope(_head_norm(k, kg_ref[...], hm), cos, sin, first_half).astype(BF16)


def _proj(x, mod, g1, w_in, qg, kg, head_mean, cos, sin, tm):
    b, s, d = x.shape
    out = jax.ShapeDtypeStruct((b, s, NA_WIDTH), BF16)
    tok = lambda t, i: (i, t, 0)
    const = lambda t, i: (0, 0)
    return pl.pallas_call(
        _proj_kernel,
        grid=(s // tm, b),
        in_specs=[pl.BlockSpec((1, tm, d), tok),
                  pl.BlockSpec((1, 8, d), lambda t, i: (i, 0, 0)),
                  pl.BlockSpec((1, d), const),
                  pl.BlockSpec(w_in.shape, const),
                  pl.BlockSpec((1, NA_WIDTH), const),
                  pl.BlockSpec((1, NA_WIDTH), const),
                  pl.BlockSpec(head_mean.shape, const),
                  pl.BlockSpec((tm, cos.shape[1]), lambda t, i: (t, 0)),
                  pl.BlockSpec((tm, sin.shape[1]), lambda t, i: (t, 0))],
        out_specs=[pl.BlockSpec((1, tm, NA_WIDTH), tok)] * 4,
        out_shape=[out] * 4,
        compiler_params=_params("arbitrary", "arbitrary"),
        name="proj",
    )(x, mod, g1, w_in, qg, kg, head_mean, cos, sin)


def _pitch(n):
    tiles = -(-n // SUBLANES)
    return SUBLANES * (tiles + 1 - tiles % 2)


def _fourier_kernel(u_ref, cs_ref, w1_ref, g_ref, y_ref, zr_ref, zi_ref, tr_ref, ti_ref, *,
                    rows, a_chunk):
    gd = FOURIER_GROUP_DIM
    w = GRID_W
    row_pitch = _pitch(rows)
    col_pitch = _pitch(w)

    def channel_stage(i, carry):
        base = pl.multiple_of(i * (a_chunk * w), a_chunk * w)
        zc = _dot(u_ref[0, pl.ds(base, a_chunk * w), :], cs_ref[...])
        for t in range(a_chunk):
            a = i * a_chunk + t
            zr_ref[pl.ds(a, w, stride=row_pitch), :] = zc[t * w:(t + 1) * w, :gd]
            zi_ref[pl.ds(a, w, stride=row_pitch), :] = zc[t * w:(t + 1) * w, gd:]
        return carry

    def row_stage(i, carry):
        cols = []
        for t in range(2):
            base = pl.multiple_of((2 * i + t) * row_pitch, SUBLANES)
            cols.append(jnp.concatenate([zr_ref[pl.ds(base, rows), :],
                                         zi_ref[pl.ds(base, rows), :]], axis=0))
        t2 = _dot(w1_ref[...], jnp.concatenate(cols, axis=1).astype(BF16))
        for t in range(2):
            b = 2 * i + t
            tr_ref[pl.ds(b, rows, stride=col_pitch), :] = t2[:rows, t * gd:(t + 1) * gd]
            ti_ref[pl.ds(b, rows, stride=col_pitch), :] = t2[rows:, t * gd:(t + 1) * gd]
        return carry

    def col_stage(ka, carry):
        base = pl.multiple_of(ka * col_pitch, SUBLANES)
        rhs = jnp.concatenate([tr_ref[pl.ds(base, w), :], ti_ref[pl.ds(base, w), :]], axis=0)
        y = _dot(g_ref[ka], rhs.astype(BF16))
        zr_ref[pl.ds(ka, w, stride=row_pitch), :] = y
        return carry

    def emit(kb, carry):
        src = pl.multiple_of(kb * row_pitch, SUBLANES)
        dst = pl.multiple_of(kb * rows, rows)
        y_ref[0, pl.ds(dst, rows), :] = zr_ref[pl.ds(src, rows), :].astype(BF16)
        return carry

    lax.fori_loop(0, rows // a_chunk, channel_stage, 0, unroll=2)
    lax.fori_loop(0, w // 2, row_stage, 0, unroll=4)
    lax.fori_loop(0, rows, col_stage, 0, unroll=8)
    lax.fori_loop(0, w, emit, 0, unroll=4)


def _fourier(f, cs, w1, g2):
    b, s, width = f.shape
    rows = s // GRID_W
    gd = FOURIER_GROUP_DIM
    slab = pl.BlockSpec((1, s, gd), lambda i, g: (i, 0, g))
    const2 = lambda i, g: (0, 0)
    return pl.pallas_call(
        functools.partial(_fourier_kernel, rows=rows, a_chunk=8),
        grid=(b, width // gd),
        in_specs=[slab,
                  pl.BlockSpec(cs.shape, const2),
                  pl.BlockSpec(w1.shape, const2),
                  pl.BlockSpec(g2.shape, lambda i, g: (0, 0, 0))],
        out_specs=slab,
        out_shape=jax.ShapeDtypeStruct(f.shape, BF16),
        scratch_shapes=[pltpu.VMEM((GRID_W * _pitch(rows), gd), F32),
                        pltpu.VMEM((GRID_W * _pitch(rows), gd), F32),
                        pltpu.VMEM((rows * _pitch(GRID_W), gd), F32),
                        pltpu.VMEM((rows * _pitch(GRID_W), gd), F32)],
        compiler_params=_params("arbitrary", "arbitrary"),
        name="fourier",
    )(f, cs, w1, g2)


def _attn_kernel(q_ref, k_ref, v_ref, kc_ref, vc_ref, bias_ref, o_ref, *, rows_per_step):
    n_rows = k_ref.shape[1] // GRID_W
    n_loc = WIN_ROWS * GRID_W
    lane = lax.broadcasted_iota(jnp.int32, (GRID_W, LANES), 1)
    low = lane < HEAD_DIM
    units = [(rr, pair) for rr in range(rows_per_step) for pair in range(NA_WIDTH // LANES)]

    def window(rr):
        r = pl.program_id(1) * rows_per_step + rr
        row_start = jnp.clip(r - WIN_ROWS // 2, 0, n_rows - WIN_ROWS)
        return r - row_start, pl.multiple_of(row_start * GRID_W, GRID_W)

    def scores(rr, pair):
        variant, start = window(rr)
        sl = slice(pair * LANES, (pair + 1) * LANES)
        qp = q_ref[0, rr * GRID_W:(rr + 1) * GRID_W, sl]
        zero = jnp.zeros_like(qp)
        q2 = jnp.concatenate([jnp.where(low, qp, zero), jnp.where(low, zero, qp)], axis=0)
        s_loc = _dot_nt(q2, k_ref[0, pl.ds(start, n_loc), sl]) + bias_ref[variant, pair]
        s_ctx = _dot_nt(q2, kc_ref[0, :, sl])
        return s_loc, s_ctx

    def finish(rr, pair, s_loc, s_ctx):
        _, start = window(rr)
        sl = slice(pair * LANES, (pair + 1) * LANES)
        m = jnp.maximum(jnp.max(s_loc, axis=-1, keepdims=True),
                        jnp.max(s_ctx, axis=-1, keepdims=True))
        p_loc = jnp.exp2(s_loc - m)
        p_ctx = jnp.exp2(s_ctx - m)
        denom = (jnp.sum(p_loc, axis=-1, keepdims=True)
                 + jnp.sum(p_ctx, axis=-1, keepdims=True))
        o = (_dot(p_loc.astype(BF16), v_ref[0, pl.ds(start, n_loc), sl])
             + _dot(p_ctx.astype(BF16), vc_ref[0, :, sl]))
        o = o / denom
        o_ref[0, rr * GRID_W:(rr + 1) * GRID_W, sl] = (
            jnp.where(low, o[:GRID_W], o[GRID_W:]).astype(BF16))

    pending = scores(*units[0])
    for i, unit in enumerate(units):
        upcoming = scores(*units[i + 1]) if i + 1 < len(units) else None
        finish(*unit, *pending)
        pending = upcoming


def _attn(q, k, v, kc, vc, bias, rows_per_step):
    b, s, width = q.shape
    n_rows = s // GRID_W
    c = kc.shape[1]
    tq = rows_per_step * GRID_W
    whole = lambda i, r: (i, 0, 0)
    return pl.pallas_call(
        functools.partial(_attn_kernel, rows_per_step=rows_per_step),
        grid=(b, n_rows // rows_per_step),
        in_specs=[pl.BlockSpec((1, tq, width), lambda i, r: (i, r, 0)),
                  pl.BlockSpec((1, s, width), whole),
                  pl.BlockSpec((1, s, width), whole),
                  pl.BlockSpec((1, c, width), whole),
                  pl.BlockSpec((1, c, width), whole),
                  pl.BlockSpec(bias.shape, lambda i, r: (0, 0, 0, 0),
                               pipeline_mode=pl.Buffered(1))],
        out_specs=pl.BlockSpec((1, tq, width), lambda i, r: (i, r, 0)),
        out_shape=jax.ShapeDtypeStruct((b, s, width), BF16),
        compiler_params=_params("arbitrary", "arbitrary"),
        name="attn",
    )(q, k, v, kc, vc, bias)


def _merge_kernel(x_ref, ya_ref, yf_ref, mod_ref, g1_ref, wg_ref, wfo_ref, wao_ref, wo_ref,
                  o_ref):
    d = g1_ref.shape[1]
    x = x_ref[0]
    m = mod_ref[0]
    h = _rms_mod(x, g1_ref[...], m[0:1], m[1:2]).astype(BF16)
    gates = jax.nn.sigmoid(_dot(h, wg_ref[...]))
    merged = (gates[:, :d] * _dot(yf_ref[0], wfo_ref[...])
              + gates[:, d:] * _dot(ya_ref[0], wao_ref[...]))
    o_ref[0] = x + m[2:3] * _dot(merged.astype(BF16), wo_ref[...])


def _merge(x, y_attn, y_four, mod, g1, w_gate, w_fo, w_ao, w_o, tm):
    b, s, d = x.shape
    const = lambda i, t: (0, 0)
    tok = lambda i, t: (i, t, 0)
    return pl.pallas_call(
        _merge_kernel,
        grid=(b, s // tm),
        in_specs=[pl.BlockSpec((1, tm, d), tok),
                  pl.BlockSpec((1, tm, NA_WIDTH), tok),
                  pl.BlockSpec((1, tm, FOURIER_WIDTH), tok),
                  pl.BlockSpec((1, 8, d), lambda i, t: (i, 0, 0)),
                  pl.BlockSpec((1, d), const),
                  pl.BlockSpec(w_gate.shape, const),
                  pl.BlockSpec(w_fo.shape, const),
                  pl.BlockSpec(w_ao.shape, const),
                  pl.BlockSpec(w_o.shape, const)],
        out_specs=pl.BlockSpec((1, tm, d), tok),
        out_shape=jax.ShapeDtypeStruct(x.shape, F32),
        compiler_params=_params("arbitrary", "arbitrary"),
        name="merge",
    )(x, y_attn, y_four, mod, g1, w_gate, w_fo, w_ao, w_o)


def _mlp_kernel(x_ref, mod_ref, g2_ref, w1_ref, w2_ref, o_ref, *, ff_chunk):
    x = x_ref[0]
    m = mod_ref[0]
    h = _rms_mod(x, g2_ref[...], m[3:4], m[4:5]).astype(BF16)
    acc = jnp.zeros(x.shape, F32)
    for c in range(w1_ref.shape[1] // ff_chunk):
        a = jnp.maximum(_dot(h, w1_ref[:, c * ff_chunk:(c + 1) * ff_chunk]), 0.0)
        acc = acc + _dot((a * a).astype(BF16), w2_ref[c * ff_chunk:(c + 1) * ff_chunk, :])
    o_ref[0] = x + m[5:6] * acc


def _mlp(x, mod, g2, w1, w2, tm, ff_chunk):
    b, s, d = x.shape
    const = lambda i, t: (0, 0)
    return pl.pallas_call(
        functools.partial(_mlp_kernel, ff_chunk=ff_chunk),
        grid=(b, s // tm),
        in_specs=[pl.BlockSpec((1, tm, d), lambda i, t: (i, t, 0)),
                  pl.BlockSpec((1, 8, d), lambda i, t: (i, 0, 0)),
                  pl.BlockSpec((1, d), const),
                  pl.BlockSpec(w1.shape, const),
                  pl.BlockSpec(w2.shape, const)],
        out_specs=pl.BlockSpec((1, tm, d), lambda i, t: (i, t, 0)),
        out_shape=jax.ShapeDtypeStruct(x.shape, F32),
        compiler_params=_params("arbitrary", "arbitrary"),
        name="mlp",
    )(x, mod, g2, w1, w2)


def _dft_tables(rows):
    n = rows * GRID_W
    gd = FOURIER_GROUP_DIM
    c = np.arange(gd)
    ang = 2.0 * np.pi * np.outer(c, c) / gd
    cs = np.concatenate([np.cos(ang), np.sin(ang)], axis=1) * 2.0 ** -3
    a = np.arange(rows)
    ang = 2.0 * np.pi * np.outer(a, a) / rows
    c1, s1 = np.cos(ang), np.sin(ang)
    w1 = np.block([[c1, -s1], [s1, c1]]) * 2.0 ** -4
    b = np.arange(GRID_W)
    k = a[:, None] + rows * b[None, :]
    ang = 2.0 * np.pi * k[:, :, None] * b[None, None, :] / n
    g2 = np.concatenate([np.cos(ang), -np.sin(ang)], axis=2) * 2.0 ** -3
    assert gd * rows * GRID_W == 2 ** 20
    return tuple(jnp.asarray(m, F32).astype(BF16) for m in (cs, w1, g2))


def _rope_tables(seq):
    t = np.arange(seq)
    axis_dim = HEAD_DIM // 2
    inv_freq = ROPE_THETA ** (-np.arange(0, axis_dim, 2, dtype=np.float64) / axis_dim)
    ang_r = (t // GRID_W)[:, None] * inv_freq
    ang_c = (t % GRID_W)[:, None] * inv_freq
    ang = np.concatenate([ang_r, ang_r, ang_c, ang_c], axis=1)
    sign = np.tile(np.repeat([-1.0, 1.0], axis_dim // 2), 2)
    reps = LANES // HEAD_DIM
    cos = np.tile(np.cos(ang), (1, reps))
    sin = np.tile(np.sin(ang) * sign, (1, reps))
    return jnp.asarray(cos, F32), jnp.asarray(sin, F32)


def _bias_table(rpb, n_rows):
    del n_rows
    c = np.arange(GRID_W)
    col_start = np.clip(c - WIN_COLS // 2, 0, GRID_W - WIN_COLS)
    kcol = np.arange(GRID_W)
    valid = (kcol[None, :] >= col_start[:, None]) & (kcol[None, :] < col_start[:, None] + WIN_COLS)
    left = GRID_W - WIN_COLS
    padded = jnp.pad(rpb, ((0, 0), (0, 0), (left, GRID_W)))
    toep = jnp.stack([padded[:, :, GRID_W - 1 - q:2 * GRID_W - 1 - q] for q in range(GRID_W)],
                     axis=1)
    toep = jnp.where(valid[None, :, None, :], toep * LOG2_E, MASK_VALUE)
    tab = jnp.stack([toep[:, :, WIN_ROWS - 1 - var:2 * WIN_ROWS - 1 - var]
                     for var in range(WIN_ROWS)], axis=0)
    heads_per_block = LANES // HEAD_DIM
    return tab.reshape(WIN_ROWS, NA_HEADS // heads_per_block, heads_per_block * GRID_W,
                       WIN_ROWS * GRID_W).astype(F32)


def kernel(x, c, ctx, c_ctx, w_ada, b_ada, norm1_g, norm2_g, w_in, q_norm_g, k_norm_g, rpb,
           w_branch_gate, w_fourier_out, w_attn_out, w_out, w_mlp1, w_mlp2):
    b, s, d = x.shape
    assert w_ada.shape[0] == 1, "single-layer block only"
    n_rows = s // GRID_W
    assert n_rows >= WIN_ROWS and b < 8

    cond = jnp.zeros((8, d), F32).at[:b].set(c).at[b].set(c_ctx)
    mods = _ada(cond, w_ada[0], b_ada)
    mod = jnp.pad(mods[:b].reshape(b, N_MOD, d), ((0, 0), (0, 8 - N_MOD), (0, 0)))
    mod_ctx = jnp.pad(mods[b].reshape(N_MOD, d), ((0, 8 - N_MOD), (0, 0)))

    half = NA_WIDTH // 2
    head_mean = jnp.asarray(
        np.kron(np.eye(half // HEAD_DIM), np.full((HEAD_DIM, HEAD_DIM), 1.0 / HEAD_DIM)),
        F32).astype(BF16)
    qg = jnp.tile(q_norm_g[0], NA_HEADS)[None, :] * (HEAD_DIM ** -0.5 * LOG2_E)
    kg = jnp.tile(k_norm_g[0], NA_HEADS)[None, :]
    cos, sin = _rope_tables(s)
    cs, w1, g2 = _dft_tables(n_rows)
    w_in_b = w_in[0].astype(BF16)

    kc, vc = _ctx_kv(ctx, mod_ctx, norm1_g, w_in_b, kg, head_mean)
    f, q, k, v = _proj(x, mod, norm1_g, w_in_b, qg, kg, head_mean, cos, sin, tm=512)
    y_four = _fourier(f, cs, w1, g2)
    y_attn = _attn(q, k, v, kc, vc, _bias_table(rpb[0], n_rows), rows_per_step=4)
    x1 = _merge(x, y_attn, y_four, mod, norm1_g, w_branch_gate[0].astype(BF16),
                w_fourier_out[0].astype(BF16), w_attn_out[0].astype(BF16),
                w_out[0].astype(BF16), tm=512)
    return _mlp(x1, mod, norm2_g, w_mlp1[0].astype(BF16), w_mlp2[0].astype(BF16),
                tm=512, ff_chunk=1024)
```

```python
import functools

import numpy as np
import jax
import jax.numpy as jnp
from jax import lax
from jax.experimental import pallas as pl
from jax.experimental.pallas import tpu as pltpu

GRID_W = 64
N_FOURIER_GROUPS = 4
FOURIER_GROUP_DIM = 128
FOURIER_WIDTH = N_FOURIER_GROUPS * FOURIER_GROUP_DIM
NA_HEADS = 8
HEAD_DIM = 64
NA_WIDTH = NA_HEADS * HEAD_DIM
WIN_ROWS = 8
WIN_COLS = 16
ROPE_THETA = 10000.0
NORM_EPS = 1e-6
N_MOD = 6
MASK_VALUE = -1e30
LOG2_E = 1.4426950408889634
SCORE_LOOKAHEAD = 2

LANES = 128
SUBLANES = 8
VMEM_LIMIT_BYTES = 48 * 1024 * 1024

BF16 = jnp.bfloat16
F32 = jnp.float32


def _dot(a, b):
    return jnp.dot(a, b, preferred_element_type=F32)


def _dot_nt(a, b):
    return lax.dot_general(a, b, (((1,), (1,)), ((), ())), preferred_element_type=F32)


def _params(*semantics):
    return pltpu.CompilerParams(dimension_semantics=semantics,
                                vmem_limit_bytes=VMEM_LIMIT_BYTES)


def _rms_mod(x, g, shift, scale):
    ms = jnp.mean(x * x, axis=-1, keepdims=True)
    return (x * lax.rsqrt(ms + NORM_EPS) * g) * (1.0 + scale) + shift


def _head_norm(a, gain, head_mean):
    half = NA_WIDTH // 2
    sq = (a * a).astype(BF16)
    ms = jnp.concatenate([_dot(sq[:, :half], head_mean), _dot(sq[:, half:], head_mean)], axis=1)
    return a * lax.rsqrt(ms + NORM_EPS) * gain


def _ada_kernel(cond_ref, w_ref, b_ref, o_ref):
    cond = cond_ref[...]
    s = cond * jax.nn.sigmoid(cond)
    s_hi = s.astype(BF16)
    s_lo = (s - s_hi.astype(F32)).astype(BF16)
    w = w_ref[...]
    w_hi = w.astype(BF16)
    w_lo = (w - w_hi.astype(F32)).astype(BF16)
    o_ref[...] = _dot(s_hi, w_hi) + _dot(s_hi, w_lo) + _dot(s_lo, w_hi) + b_ref[...]


def _ada(cond, w, b):
    rows, d = cond.shape
    n = w.shape[1]
    tn = 768
    return pl.pallas_call(
        _ada_kernel,
        grid=(n // tn,),
        in_specs=[pl.BlockSpec((rows, d), lambda j: (0, 0)),
                  pl.BlockSpec((d, tn), lambda j: (0, j)),
                  pl.BlockSpec((1, tn), lambda j: (0, j))],
        out_specs=pl.BlockSpec((rows, tn), lambda j: (0, j)),
        out_shape=jax.ShapeDtypeStruct((rows, n), F32),
        compiler_params=_params("arbitrary"),
        name="ada",
    )(cond, w, b)


def _ctx_kv_kernel(ctx_ref, mod_ref, g1_ref, w_ref, kg_ref, hm_ref, k_ref, v_ref):
    m = mod_ref[...]
    h = _rms_mod(ctx_ref[0], g1_ref[...], m[0:1], m[1:2])
    p = _dot(h.astype(BF16), w_ref[...])
    k_ref[0] = _head_norm(p[:, :NA_WIDTH], kg_ref[...], hm_ref[...]).astype(BF16)
    v_ref[0] = p[:, NA_WIDTH:].astype(BF16)


def _ctx_kv(ctx, mod_ctx, g1, w_in, kg, head_mean):
    b, c, d = ctx.shape
    kv_cols = 2 * NA_WIDTH
    out = jax.ShapeDtypeStruct((b, c, NA_WIDTH), BF16)
    return pl.pallas_call(
        _ctx_kv_kernel,
        grid=(b,),
        in_specs=[pl.BlockSpec((1, c, d), lambda i: (i, 0, 0)),
                  pl.BlockSpec((8, d), lambda i: (0, 0)),
                  pl.BlockSpec((1, d), lambda i: (0, 0)),
                  pl.BlockSpec((d, kv_cols), lambda i: (0, (w_in.shape[1] - kv_cols) // kv_cols)),
                  pl.BlockSpec((1, NA_WIDTH), lambda i: (0, 0)),
                  pl.BlockSpec(head_mean.shape, lambda i: (0, 0))],
        out_specs=[pl.BlockSpec((1, c, NA_WIDTH), lambda i: (i, 0, 0))] * 2,
        out_shape=[out, out],
        compiler_params=_params("arbitrary"),
        name="ctx_kv",
    )(ctx, mod_ctx, g1, w_in, kg, head_mean)


def _rope(a, cos, sin_signed, first_half):
    half = HEAD_DIM // 4
    partner = jnp.where(first_half, pltpu.roll(a, LANES - half, 1), pltpu.roll(a, half, 1))
    return a * cos + partner * sin_signed


def _proj_kernel(x_ref, mod_ref, g1_ref, w_ref, qg_ref, kg_ref, hm_ref, cos_ref, sin_ref,
                 f_ref, q_ref, k_ref, v_ref, *, chunk):
    m = mod_ref[0]
    hm = hm_ref[...]
    lane = lax.broadcasted_iota(jnp.int32, (chunk, LANES), 1)
    first_half = (lane % (HEAD_DIM // 2)) < (HEAD_DIM // 4)
    qk = ((FOURIER_WIDTH, qg_ref, q_ref), (FOURIER_WIDTH + NA_WIDTH, kg_ref, k_ref))

    def project(c):
        rows = slice(c * chunk, (c + 1) * chunk)
        h = _rms_mod(x_ref[0, rows, :], g1_ref[...], m[0:1], m[1:2])
        return _dot(h.astype(BF16), w_ref[...])

    def finish(c, p):
        rows = slice(c * chunk, (c + 1) * chunk)
        f_ref[0, rows, :] = p[:, :FOURIER_WIDTH].astype(BF16)
        v_ref[0, rows, :] = p[:, FOURIER_WIDTH + 2 * NA_WIDTH:].astype(BF16)
        cos = cos_ref[rows, :]
        sin = sin_ref[rows, :]
        for col, gain_ref, dst in qk:
            a = _head_norm(p[:, col:col + NA_WIDTH], gain_ref[...], hm)
            blocks = [_rope(a[:, j:j + LANES], cos, sin, first_half)
                      for j in range(0, NA_WIDTH, LANES)]
            dst[0, rows, :] = jnp.concatenate(blocks, axis=1).astype(BF16)

    n_chunks = x_ref.shape[1] // chunk
    pending = project(0)
    for c in range(n_chunks):
        upcoming = project(c + 1) if c + 1 < n_chunks else None
        finish(c, pending)
        pending = upcoming


def _proj(x, mod, g1, w_in, qg, kg, head_mean, cos, sin, tm):
    b, s, d = x.shape
    out = jax.ShapeDtypeStruct((b, s, NA_WIDTH), BF16)
    tok = lambda t, i: (i, t, 0)
    const = lambda t, i: (0, 0)
    return pl.pallas_call(
        functools.partial(_proj_kernel, chunk=128),
        grid=(s // tm, b),
        in_specs=[pl.BlockSpec((1, tm, d), tok),
                  pl.BlockSpec((1, 8, d), lambda t, i: (i, 0, 0)),
                  pl.BlockSpec((1, d), const),
                  pl.BlockSpec(w_in.shape, const),
                  pl.BlockSpec((1, NA_WIDTH), const),
                  pl.BlockSpec((1, NA_WIDTH), const),
                  pl.BlockSpec(head_mean.shape, const),
                  pl.BlockSpec((tm, cos.shape[1]), lambda t, i: (t, 0)),
                  pl.BlockSpec((tm, sin.shape[1]), lambda t, i: (t, 0))],
        out_specs=[pl.BlockSpec((1, tm, NA_WIDTH), tok)] * 4,
        out_shape=[out] * 4,
        compiler_params=_params("arbitrary", "arbitrary"),
        name="proj",
    )(x, mod, g1, w_in, qg, kg, head_mean, cos, sin)


def _pitch(n):
    tiles = -(-n // SUBLANES)
    return SUBLANES * (tiles + 1 - tiles % 2)


def _fourier_kernel(u_ref, cs_ref, w1_ref, g_ref, y_ref, ut_ref, z_ref, tr_ref, ti_ref, *,
                    rows, b_chunk):
    gd = FOURIER_GROUP_DIM
    w = GRID_W
    row_pitch = _pitch(rows)
    col_pitch = _pitch(w)

    def reorder(a, carry):
        base = pl.multiple_of(a * w, w)
        ut_ref[pl.ds(a, w, stride=row_pitch), :] = u_ref[0, pl.ds(base, w), :].astype(F32)
        return carry

    def clear_pad(b, carry):
        base = pl.multiple_of(b * row_pitch + rows, SUBLANES)
        ut_ref[pl.ds(base, row_pitch - rows), :] = jnp.zeros((row_pitch - rows, gd), F32)
        return carry

    def channel_stage(i, carry):
        base = pl.multiple_of(i * (b_chunk * row_pitch), SUBLANES)
        zc = _dot(ut_ref[pl.ds(base, b_chunk * row_pitch), :].astype(BF16), cs_ref[...])
        for t in range(b_chunk):
            b = i * b_chunk + t
            z_ref[b, :rows] = zc[t * row_pitch:t * row_pitch + rows, :gd].astype(BF16)
            z_ref[b, rows:] = zc[t * row_pitch:t * row_pitch + rows, gd:].astype(BF16)
        return carry

    def row_stage(i, carry):
        t2 = _dot(w1_ref[...], jnp.concatenate([z_ref[2 * i], z_ref[2 * i + 1]], axis=1))
        for t in range(2):
            b = 2 * i + t
            tr_ref[pl.ds(b, rows, stride=col_pitch), :] = t2[:rows, t * gd:(t + 1) * gd]
            ti_ref[pl.ds(b, rows, stride=col_pitch), :] = t2[rows:, t * gd:(t + 1) * gd]
        return carry

    def col_stage(ka, carry):
        base = pl.multiple_of(ka * col_pitch, SUBLANES)
        rhs = jnp.concatenate([tr_ref[pl.ds(base, w), :], ti_ref[pl.ds(base, w), :]], axis=0)
        y = _dot(g_ref[ka], rhs.astype(BF16))
        ut_ref[pl.ds(ka, w, stride=row_pitch), :] = y
        return carry

    def emit(kb, carry):
        src = pl.multiple_of(kb * row_pitch, SUBLANES)
        dst = pl.multiple_of(kb * rows, rows)
        y_ref[0, pl.ds(dst, rows), :] = ut_ref[pl.ds(src, rows), :].astype(BF16)
        return carry

    lax.fori_loop(0, w, clear_pad, 0, unroll=8)
    lax.fori_loop(0, rows, reorder, 0, unroll=8)
    lax.fori_loop(0, w // b_chunk, channel_stage, 0)
    lax.fori_loop(0, w // 2, row_stage, 0, unroll=4)
    lax.fori_loop(0, rows, col_stage, 0, unroll=16)
    lax.fori_loop(0, w, emit, 0, unroll=4)


def _fourier(f, cs, w1, g2):
    b, s, width = f.shape
    rows = s // GRID_W
    gd = FOURIER_GROUP_DIM
    slab = pl.BlockSpec((1, s, gd), lambda i, g: (i, 0, g))
    const2 = lambda i, g: (0, 0)
    return pl.pallas_call(
        functools.partial(_fourier_kernel, rows=rows, b_chunk=8),
        grid=(b, width // gd),
        in_specs=[slab,
                  pl.BlockSpec(cs.shape, const2),
                  pl.BlockSpec(w1.shape, const2),
                  pl.BlockSpec(g2.shape, lambda i, g: (0, 0, 0))],
        out_specs=slab,
        out_shape=jax.ShapeDtypeStruct(f.shape, BF16),
        scratch_shapes=[pltpu.VMEM((GRID_W * _pitch(rows), gd), F32),
                        pltpu.VMEM((GRID_W, 2 * rows, gd), BF16),
                        pltpu.VMEM((rows * _pitch(GRID_W), gd), F32),
                        pltpu.VMEM((rows * _pitch(GRID_W), gd), F32)],
        compiler_params=_params("arbitrary", "arbitrary"),
        name="fourier",
    )(f, cs, w1, g2)


def _attn_kernel(q_ref, k_ref, v_ref, kc_ref, vc_ref, bias_ref, o_ref, *, rows_per_step):
    n_rows = k_ref.shape[1] // GRID_W
    n_loc = WIN_ROWS * GRID_W
    lane = lax.broadcasted_iota(jnp.int32, (GRID_W, LANES), 1)
    low = lane < HEAD_DIM
    units = [(rr, pair) for rr in range(rows_per_step) for pair in range(NA_WIDTH // LANES)]

    def window(rr):
        r = pl.program_id(1) * rows_per_step + rr
        row_start = jnp.clip(r - WIN_ROWS // 2, 0, n_rows - WIN_ROWS)
        return r - row_start, pl.multiple_of(row_start * GRID_W, GRID_W)

    def scores(rr, pair):
        variant, start = window(rr)
        sl = slice(pair * LANES, (pair + 1) * LANES)
        qp = q_ref[0, rr * GRID_W:(rr + 1) * GRID_W, sl]
        zero = jnp.zeros_like(qp)
        q2 = jnp.concatenate([jnp.where(low, qp, zero), jnp.where(low, zero, qp)], axis=0)
        s_loc = _dot_nt(q2, k_ref[0, pl.ds(start, n_loc), sl]) + bias_ref[variant, pair]
        s_ctx = _dot_nt(q2, kc_ref[0, :, sl])
        return s_loc, s_ctx

    def finish(rr, pair, s_loc, s_ctx):
        _, start = window(rr)
        sl = slice(pair * LANES, (pair + 1) * LANES)
        m = jnp.maximum(jnp.max(s_loc, axis=-1, keepdims=True),
                        jnp.max(s_ctx, axis=-1, keepdims=True))
        p_loc = jnp.exp2(s_loc - m)
        p_ctx = jnp.exp2(s_ctx - m)
        denom = (jnp.sum(p_loc, axis=-1, keepdims=True)
                 + jnp.sum(p_ctx, axis=-1, keepdims=True))
        o = (_dot(p_loc.astype(BF16), v_ref[0, pl.ds(start, n_loc), sl])
             + _dot(p_ctx.astype(BF16), vc_ref[0, :, sl]))
        o = o / denom
        o_ref[0, rr * GRID_W:(rr + 1) * GRID_W, sl] = (
            jnp.where(low, o[:GRID_W], o[GRID_W:]).astype(BF16))

    pending = [scores(*u) for u in units[:SCORE_LOOKAHEAD]]
    for i, unit in enumerate(units):
        if i + SCORE_LOOKAHEAD < len(units):
            pending.append(scores(*units[i + SCORE_LOOKAHEAD]))
        finish(*unit, *pending.pop(0))


def _attn(q, k, v, kc, vc, bias, rows_per_step):
    b, s, width = q.shape
    n_rows = s // GRID_W
    c = kc.shape[1]
    tq = rows_per_step * GRID_W
    whole = lambda i, r: (i, 0, 0)
    return pl.pallas_call(
        functools.partial(_attn_kernel, rows_per_step=rows_per_step),
        grid=(b, n_rows // rows_per_step),
        in_specs=[pl.BlockSpec((1, tq, width), lambda i, r: (i, r, 0)),
                  pl.BlockSpec((1, s, width), whole),
                  pl.BlockSpec((1, s, width), whole),
                  pl.BlockSpec((1, c, width), whole),
                  pl.BlockSpec((1, c, width), whole),
                  pl.BlockSpec(bias.shape, lambda i, r: (0, 0, 0, 0),
                               pipeline_mode=pl.Buffered(1))],
        out_specs=pl.BlockSpec((1, tq, width), lambda i, r: (i, r, 0)),
        out_shape=jax.ShapeDtypeStruct((b, s, width), BF16),
        compiler_params=_params("arbitrary", "arbitrary"),
        name="attn",
    )(q, k, v, kc, vc, bias)


def _merge_kernel(x_ref, ya_ref, yf_ref, mod_ref, g1_ref, wg_ref, wfo_ref, wao_ref, wo_ref,
                  o_ref):
    d = g1_ref.shape[1]
    x = x_ref[0]
    m = mod_ref[0]
    h = _rms_mod(x, g1_ref[...], m[0:1], m[1:2]).astype(BF16)
    gates = jax.nn.sigmoid(_dot(h, wg_ref[...]))
    merged = (gates[:, :d] * _dot(yf_ref[0], wfo_ref[...])
              + gates[:, d:] * _dot(ya_ref[0], wao_ref[...]))
    o_ref[0] = x + m[2:3] * _dot(merged.astype(BF16), wo_ref[...])


def _merge(x, y_attn, y_four, mod, g1, w_gate, w_fo, w_ao, w_o, tm):
    b, s, d = x.shape
    const = lambda i, t: (0, 0)
    tok = lambda i, t: (i, t, 0)
    return pl.pallas_call(
        _merge_kernel,
        grid=(b, s // tm),
        in_specs=[pl.BlockSpec((1, tm, d), tok),
                  pl.BlockSpec((1, tm, NA_WIDTH), tok),
                  pl.BlockSpec((1, tm, FOURIER_WIDTH), tok),
                  pl.BlockSpec((1, 8, d), lambda i, t: (i, 0, 0)),
                  pl.BlockSpec((1, d), const),
                  pl.BlockSpec(w_gate.shape, const),
                  pl.BlockSpec(w_fo.shape, const),
                  pl.BlockSpec(w_ao.shape, const),
                  pl.BlockSpec(w_o.shape, const)],
        out_specs=pl.BlockSpec((1, tm, d), tok),
        out_shape=jax.ShapeDtypeStruct(x.shape, F32),
        compiler_params=_params("arbitrary", "arbitrary"),
        name="merge",
    )(x, y_attn, y_four, mod, g1, w_gate, w_fo, w_ao, w_o)


def _mlp_kernel(x_ref, mod_ref, g2_ref, w1_ref, w2_ref, o_ref, *, ff_chunk):
    x = x_ref[0]
    m = mod_ref[0]
    h = _rms_mod(x, g2_ref[...], m[3:4], m[4:5]).astype(BF16)
    acc = jnp.zeros(x.shape, F32)
    for c in range(w1_ref.shape[1] // ff_chunk):
        a = jnp.maximum(_dot(h, w1_ref[:, c * ff_chunk:(c + 1) * ff_chunk]), 0.0)
        acc = acc + _dot((a * a).astype(BF16), w2_ref[c * ff_chunk:(c + 1) * ff_chunk, :])
    o_ref[0] = x + m[5:6] * acc


def _mlp(x, mod, g2, w1, w2, tm, ff_chunk):
    b, s, d = x.shape
    const = lambda i, t: (0, 0)
    return pl.pallas_call(
        functools.partial(_mlp_kernel, ff_chunk=ff_chunk),
        grid=(b, s // tm),
        in_specs=[pl.BlockSpec((1, tm, d), lambda i, t: (i, t, 0)),
                  pl.BlockSpec((1, 8, d), lambda i, t: (i, 0, 0)),
                  pl.BlockSpec((1, d), const),
                  pl.BlockSpec(w1.shape, const),
                  pl.BlockSpec(w2.shape, const)],
        out_specs=pl.BlockSpec((1, tm, d), lambda i, t: (i, t, 0)),
        out_shape=jax.ShapeDtypeStruct(x.shape, F32),
        compiler_params=_params("arbitrary", "arbitrary"),
        name="mlp",
    )(x, mod, g2, w1, w2)


def _dft_tables(rows):
    n = rows * GRID_W
    gd = FOURIER_GROUP_DIM
    c = np.arange(gd)
    ang = 2.0 * np.pi * np.outer(c, c) / gd
    cs = np.concatenate([np.cos(ang), np.sin(ang)], axis=1) * 2.0 ** -3
    a = np.arange(rows)
    ang = 2.0 * np.pi * np.outer(a, a) / rows
    c1, s1 = np.cos(ang), np.sin(ang)
    w1 = np.block([[c1, -s1], [s1, c1]]) * 2.0 ** -4
    b = np.arange(GRID_W)
    k = a[:, None] + rows * b[None, :]
    ang = 2.0 * np.pi * k[:, :, None] * b[None, None, :] / n
    g2 = np.concatenate([np.cos(ang), -np.sin(ang)], axis=2) * 2.0 ** -3
    assert gd * rows * GRID_W == 2 ** 20
    return tuple(jnp.asarray(m, F32).astype(BF16) for m in (cs, w1, g2))


def _rope_tables(seq):
    t = np.arange(seq)
    axis_dim = HEAD_DIM // 2
    inv_freq = ROPE_THETA ** (-np.arange(0, axis_dim, 2, dtype=np.float64) / axis_dim)
    ang_r = (t // GRID_W)[:, None] * inv_freq
    ang_c = (t % GRID_W)[:, None] * inv_freq
    ang = np.concatenate([ang_r, ang_r, ang_c, ang_c], axis=1)
    sign = np.tile(np.repeat([-1.0, 1.0], axis_dim // 2), 2)
    reps = LANES // HEAD_DIM
    cos = np.tile(np.cos(ang), (1, reps))
    sin = np.tile(np.sin(ang) * sign, (1, reps))
    return jnp.asarray(cos, F32), jnp.asarray(sin, F32)


def _bias_table(rpb, n_rows):
    del n_rows
    c = np.arange(GRID_W)
    col_start = np.clip(c - WIN_COLS // 2, 0, GRID_W - WIN_COLS)
    kcol = np.arange(GRID_W)
    valid = (kcol[None, :] >= col_start[:, None]) & (kcol[None, :] < col_start[:, None] + WIN_COLS)
    left = GRID_W - WIN_COLS
    padded = jnp.pad(rpb, ((0, 0), (0, 0), (left, GRID_W)))
    toep = jnp.stack([padded[:, :, GRID_W - 1 - q:2 * GRID_W - 1 - q] for q in range(GRID_W)],
                     axis=1)
    toep = jnp.where(valid[None, :, None, :], toep * LOG2_E, MASK_VALUE)
    tab = jnp.stack([toep[:, :, WIN_ROWS - 1 - var:2 * WIN_ROWS - 1 - var]
                     for var in range(WIN_ROWS)], axis=0)
    heads_per_block = LANES // HEAD_DIM
    return tab.reshape(WIN_ROWS, NA_HEADS // heads_per_block, heads_per_block * GRID_W,
                       WIN_ROWS * GRID_W).astype(F32)


def kernel(x, c, ctx, c_ctx, w_ada, b_ada, norm1_g, norm2_g, w_in, q_norm_g, k_norm_g, rpb,
           w_branch_gate, w_fourier_out, w_attn_out, w_out, w_mlp1, w_mlp2):
    b, s, d = x.shape
    assert w_ada.shape[0] == 1, "single-layer block only"
    n_rows = s // GRID_W
    assert n_rows >= WIN_ROWS and b < 8

    cond = jnp.zeros((8, d), F32).at[:b].set(c).at[b].set(c_ctx)
    mods = _ada(cond, w_ada[0], b_ada)
    mod = jnp.pad(mods[:b].reshape(b, N_MOD, d), ((0, 0), (0, 8 - N_MOD), (0, 0)))
    mod_ctx = jnp.pad(mods[b].reshape(N_MOD, d), ((0, 8 - N_MOD), (0, 0)))

    half = NA_WIDTH // 2
    head_mean = jnp.asarray(
        np.kron(np.eye(half // HEAD_DIM), np.full((HEAD_DIM, HEAD_DIM), 1.0 / HEAD_DIM)),
        F32).astype(BF16)
    qg = jnp.tile(q_norm_g[0], NA_HEADS)[None, :] * (HEAD_DIM ** -0.5 * LOG2_E)
    kg = jnp.tile(k_norm_g[0], NA_HEADS)[None, :]
    cos, sin = _rope_tables(s)
    cs, w1, g2 = _dft_tables(n_rows)
    w_in_b = w_in[0].astype(BF16)

    kc, vc = _ctx_kv(ctx, mod_ctx, norm1_g, w_in_b, kg, head_mean)
    f, q, k, v = _proj(x, mod, norm1_g, w_in_b, qg, kg, head_mean, cos, sin, tm=1024)
    y_four = _fourier(f, cs, w1, g2)
    y_attn = _attn(q, k, v, kc, vc, _bias_table(rpb[0], n_rows), rows_per_step=4)
    x1 = _merge(x, y_attn, y_four, mod, norm1_g, w_branch_gate[0].astype(BF16),
                w_fourier_out[0].astype(BF16), w_attn_out[0].astype(BF16),
                w_out[0].astype(BF16), tm=512)
    return _mlp(x1, mod, norm2_g, w_mlp1[0].astype(BF16), w_mlp2[0].astype(BF16),
                tm=512, ff_chunk=1024)
```

```python
import functools

import numpy as np
import jax
import jax.numpy as jnp
from jax import lax
from jax.experimental import pallas as pl
from jax.experimental.pallas import tpu as pltpu

GRID_W = 64
N_FOURIER_GROUPS = 4
FOURIER_GROUP_DIM = 128
FOURIER_WIDTH = N_FOURIER_GROUPS * FOURIER_GROUP_DIM
NA_HEADS = 8
HEAD_DIM = 64
NA_WIDTH = NA_HEADS * HEAD_DIM
WIN_ROWS = 8
WIN_COLS = 16
ROPE_THETA = 10000.0
NORM_EPS = 1e-6
N_MOD = 6
MASK_VALUE = -1e30
LOG2_E = 1.4426950408889634
SCORE_LOOKAHEAD = 2

LANES = 128
SUBLANES = 8
VMEM_LIMIT_BYTES = 48 * 1024 * 1024

BF16 = jnp.bfloat16
F32 = jnp.float32


def _dot(a, b):
    return jnp.dot(a, b, preferred_element_type=F32)


def _dot_nt(a, b):
    return lax.dot_general(a, b, (((1,), (1,)), ((), ())), preferred_element_type=F32)


def _params(*semantics):
    return pltpu.CompilerParams(dimension_semantics=semantics,
                                vmem_limit_bytes=VMEM_LIMIT_BYTES)


def _rms_mod(x, g, shift, scale):
    ms = jnp.mean(x * x, axis=-1, keepdims=True)
    return (x * lax.rsqrt(ms + NORM_EPS) * g) * (1.0 + scale) + shift


def _head_norm(a, gain, head_mean):
    half = NA_WIDTH // 2
    sq = (a * a).astype(BF16)
    ms = jnp.concatenate([_dot(sq[:, :half], head_mean), _dot(sq[:, half:], head_mean)], axis=1)
    return a * lax.rsqrt(ms + NORM_EPS) * gain


def _ada_kernel(cond_ref, w_ref, b_ref, o_ref):
    cond = cond_ref[...]
    s = cond * jax.nn.sigmoid(cond)
    s_hi = s.astype(BF16)
    s_lo = (s - s_hi.astype(F32)).astype(BF16)
    w = w_ref[...]
    w_hi = w.astype(BF16)
    w_lo = (w - w_hi.astype(F32)).astype(BF16)
    o_ref[...] = _dot(s_hi, w_hi) + _dot(s_hi, w_lo) + _dot(s_lo, w_hi) + b_ref[...]


def _ada(cond, w, b):
    rows, d = cond.shape
    n = w.shape[1]
    tn = 768
    return pl.pallas_call(
        _ada_kernel,
        grid=(n // tn,),
        in_specs=[pl.BlockSpec((rows, d), lambda j: (0, 0)),
                  pl.BlockSpec((d, tn), lambda j: (0, j)),
                  pl.BlockSpec((1, tn), lambda j: (0, j))],
        out_specs=pl.BlockSpec((rows, tn), lambda j: (0, j)),
        out_shape=jax.ShapeDtypeStruct((rows, n), F32),
        compiler_params=_params("arbitrary"),
        name="ada",
    )(cond, w, b)


def _ctx_kv_kernel(ctx_ref, mod_ref, g1_ref, w_ref, kg_ref, hm_ref, k_ref, v_ref):
    m = mod_ref[...]
    h = _rms_mod(ctx_ref[0], g1_ref[...], m[0:1], m[1:2])
    p = _dot(h.astype(BF16), w_ref[...])
    k_ref[0] = _head_norm(p[:, :NA_WIDTH], kg_ref[...], hm_ref[...]).astype(BF16)
    v_ref[0] = p[:, NA_WIDTH:].astype(BF16)


def _ctx_kv(ctx, mod_ctx, g1, w_in, kg, head_mean):
    b, c, d = ctx.shape
    kv_cols = 2 * NA_WIDTH
    out = jax.ShapeDtypeStruct((b, c, NA_WIDTH), BF16)
    return pl.pallas_call(
        _ctx_kv_kernel,
        grid=(b,),
        in_specs=[pl.BlockSpec((1, c, d), lambda i: (i, 0, 0)),
                  pl.BlockSpec((8, d), lambda i: (0, 0)),
                  pl.BlockSpec((1, d), lambda i: (0, 0)),
                  pl.BlockSpec((d, kv_cols), lambda i: (0, (w_in.shape[1] - kv_cols) // kv_cols)),
                  pl.BlockSpec((1, NA_WIDTH), lambda i: (0, 0)),
                  pl.BlockSpec(head_mean.shape, lambda i: (0, 0))],
        out_specs=[pl.BlockSpec((1, c, NA_WIDTH), lambda i: (i, 0, 0))] * 2,
        out_shape=[out, out],
        compiler_params=_params("arbitrary"),
        name="ctx_kv",
    )(ctx, mod_ctx, g1, w_in, kg, head_mean)


def _rope(a, cos, sin_signed, first_half):
    half = HEAD_DIM // 4
    partner = jnp.where(first_half, pltpu.roll(a, LANES - half, 1), pltpu.roll(a, half, 1))
    return a * cos + partner * sin_signed


def _proj_kernel(x_ref, mod_ref, g1_ref, w_ref, qg_ref, kg_ref, hm_ref, cos_ref, sin_ref,
                 f_ref, q_ref, k_ref, v_ref, *, chunk):
    m = mod_ref[0]
    hm = hm_ref[...]
    lane = lax.broadcasted_iota(jnp.int32, (chunk, LANES), 1)
    first_half = (lane % (HEAD_DIM // 2)) < (HEAD_DIM // 4)
    qk = ((FOURIER_WIDTH, qg_ref, q_ref), (FOURIER_WIDTH + NA_WIDTH, kg_ref, k_ref))

    def project(c):
        rows = slice(c * chunk, (c + 1) * chunk)
        h = _rms_mod(x_ref[0, rows, :], g1_ref[...], m[0:1], m[1:2])
        return _dot(h.astype(BF16), w_ref[...])

    def finish(c, p):
        rows = slice(c * chunk, (c + 1) * chunk)
        f_ref[0, rows, :] = p[:, :FOURIER_WIDTH].astype(BF16)
        v_ref[0, rows, :] = p[:, FOURIER_WIDTH + 2 * NA_WIDTH:].astype(BF16)
        cos = cos_ref[rows, :]
        sin = sin_ref[rows, :]
        for col, gain_ref, dst in qk:
            a = _head_norm(p[:, col:col + NA_WIDTH], gain_ref[...], hm)
            blocks = [_rope(a[:, j:j + LANES], cos, sin, first_half)
                      for j in range(0, NA_WIDTH, LANES)]
            dst[0, rows, :] = jnp.concatenate(blocks, axis=1).astype(BF16)

    n_chunks = x_ref.shape[1] // chunk
    pending = project(0)
    for c in range(n_chunks):
        upcoming = project(c + 1) if c + 1 < n_chunks else None
        finish(c, pending)
        pending = upcoming


def _proj(x, mod, g1, w_in, qg, kg, head_mean, cos, sin, tm):
    b, s, d = x.shape
    out = jax.ShapeDtypeStruct((b, s, NA_WIDTH), BF16)
    tok = lambda t, i: (i, t, 0)
    const = lambda t, i: (0, 0)
    return pl.pallas_call(
        functools.partial(_proj_kernel, chunk=128),
        grid=(s // tm, b),
        in_specs=[pl.BlockSpec((1, tm, d), tok),
                  pl.BlockSpec((1, 8, d), lambda t, i: (i, 0, 0)),
                  pl.BlockSpec((1, d), const),
                  pl.BlockSpec(w_in.shape, const),
                  pl.BlockSpec((1, NA_WIDTH), const),
                  pl.BlockSpec((1, NA_WIDTH), const),
                  pl.BlockSpec(head_mean.shape, const),
                  pl.BlockSpec((tm, cos.shape[1]), lambda t, i: (t, 0)),
                  pl.BlockSpec((tm, sin.shape[1]), lambda t, i: (t, 0))],
        out_specs=[pl.BlockSpec((1, tm, NA_WIDTH), tok)] * 4,
        out_shape=[out] * 4,
        compiler_params=_params("arbitrary", "arbitrary"),
        name="proj",
    )(x, mod, g1, w_in, qg, kg, head_mean, cos, sin)


def _pitch(n):
    tiles = -(-n // SUBLANES)
    return SUBLANES * (tiles + 1 - tiles % 2)


def _fourier_kernel(u_ref, cs_ref, w1_ref, g_ref, y_ref, ut_ref, z_ref, tr_ref, ti_ref, *,
                    rows, b_chunk):
    gd = FOURIER_GROUP_DIM
    w = GRID_W
    row_pitch = _pitch(rows)
    col_pitch = _pitch(w)

    def reorder(a, carry):
        base = pl.multiple_of(a * w, w)
        ut_ref[pl.ds(a, w, stride=row_pitch), :] = u_ref[0, pl.ds(base, w), :].astype(F32)
        return carry

    def clear_pad(b, carry):
        base = pl.multiple_of(b * row_pitch + rows, SUBLANES)
        ut_ref[pl.ds(base, row_pitch - rows), :] = jnp.zeros((row_pitch - rows, gd), F32)
        return carry

    def channel_stage(i, carry):
        base = pl.multiple_of(i * (b_chunk * row_pitch), SUBLANES)
        zc = _dot(ut_ref[pl.ds(base, b_chunk * row_pitch), :].astype(BF16), cs_ref[...])
        for t in range(b_chunk):
            b = i * b_chunk + t
            z_ref[b, :rows] = zc[t * row_pitch:t * row_pitch + rows, :gd].astype(BF16)
            z_ref[b, rows:] = zc[t * row_pitch:t * row_pitch + rows, gd:].astype(BF16)
        return carry

    def row_stage(i, carry):
        t2 = _dot(w1_ref[...], jnp.concatenate([z_ref[2 * i], z_ref[2 * i + 1]], axis=1))
        for t in range(2):
            b = 2 * i + t
            tr_ref[pl.ds(b, rows, stride=col_pitch), :] = t2[:rows, t * gd:(t + 1) * gd]
            ti_ref[pl.ds(b, rows, stride=col_pitch), :] = t2[rows:, t * gd:(t + 1) * gd]
        return carry

    def col_stage(ka, carry):
        base = pl.multiple_of(ka * col_pitch, SUBLANES)
        rhs = jnp.concatenate([tr_ref[pl.ds(base, w), :], ti_ref[pl.ds(base, w), :]], axis=0)
        y = _dot(g_ref[ka], rhs.astype(BF16))
        ut_ref[pl.ds(ka, w, stride=row_pitch), :] = y
        return carry

    def emit(kb, carry):
        src = pl.multiple_of(kb * row_pitch, SUBLANES)
        dst = pl.multiple_of(kb * rows, rows)
        y_ref[0, pl.ds(dst, rows), :] = ut_ref[pl.ds(src, rows), :].astype(BF16)
        return carry

    lax.fori_loop(0, w, clear_pad, 0, unroll=8)
    lax.fori_loop(0, rows, reorder, 0, unroll=8)
    lax.fori_loop(0, w // b_chunk, channel_stage, 0)
    lax.fori_loop(0, w // 2, row_stage, 0, unroll=4)
    lax.fori_loop(0, rows, col_stage, 0, unroll=16)
    lax.fori_loop(0, w, emit, 0, unroll=4)


def _fourier(f, cs, w1, g2):
    b, s, width = f.shape
    rows = s // GRID_W
    gd = FOURIER_GROUP_DIM
    slab = pl.BlockSpec((1, s, gd), lambda i, g: (i, 0, g))
    const2 = lambda i, g: (0, 0)
    return pl.pallas_call(
        functools.partial(_fourier_kernel, rows=rows, b_chunk=8),
        grid=(b, width // gd),
        in_specs=[slab,
                  pl.BlockSpec(cs.shape, const2),
                  pl.BlockSpec(w1.shape, const2),
                  pl.BlockSpec(g2.shape, lambda i, g: (0, 0, 0))],
        out_specs=slab,
        out_shape=jax.ShapeDtypeStruct(f.shape, BF16),
        scratch_shapes=[pltpu.VMEM((GRID_W * _pitch(rows), gd), F32),
                        pltpu.VMEM((GRID_W, 2 * rows, gd), BF16),
                        pltpu.VMEM((rows * _pitch(GRID_W), gd), F32),
                        pltpu.VMEM((rows * _pitch(GRID_W), gd), F32)],
        compiler_params=_params("arbitrary", "arbitrary"),
        name="fourier",
    )(f, cs, w1, g2)


def _attn_kernel(q_ref, k_ref, v_ref, kc_ref, vc_ref, bias_ref, o_ref, *, rows_per_step):
    n_rows = k_ref.shape[1] // GRID_W
    n_loc = WIN_ROWS * GRID_W
    lane = lax.broadcasted_iota(jnp.int32, (GRID_W, LANES), 1)
    low = lane < HEAD_DIM
    units = [(rr, pair) for rr in range(rows_per_step) for pair in range(NA_WIDTH // LANES)]

    def window(rr):
        r = pl.program_id(1) * rows_per_step + rr
        row_start = jnp.clip(r - WIN_ROWS // 2, 0, n_rows - WIN_ROWS)
        return r - row_start, pl.multiple_of(row_start * GRID_W, GRID_W)

    def scores(rr, pair):
        variant, start = window(rr)
        sl = slice(pair * LANES, (pair + 1) * LANES)
        qp = q_ref[0, rr * GRID_W:(rr + 1) * GRID_W, sl]
        zero = jnp.zeros_like(qp)
        q2 = jnp.concatenate([jnp.where(low, qp, zero), jnp.where(low, zero, qp)], axis=0)
        s_loc = _dot_nt(q2, k_ref[0, pl.ds(start, n_loc), sl]) + bias_ref[variant, pair]
        s_ctx = _dot_nt(q2, kc_ref[0, :, sl])
        return s_loc, s_ctx

    def finish(rr, pair, s_loc, s_ctx):
        _, start = window(rr)
        sl = slice(pair * LANES, (pair + 1) * LANES)
        m = jnp.maximum(jnp.max(s_loc, axis=-1, keepdims=True),
                        jnp.max(s_ctx, axis=-1, keepdims=True))
        p_loc = jnp.exp2(s_loc - m)
        p_ctx = jnp.exp2(s_ctx - m)
        denom = (jnp.sum(p_loc, axis=-1, keepdims=True)
                 + jnp.sum(p_ctx, axis=-1, keepdims=True))
        o = (_dot(p_loc.astype(BF16), v_ref[0, pl.ds(start, n_loc), sl])
             + _dot(p_ctx.astype(BF16), vc_ref[0, :, sl]))
        o = o / denom
        o_ref[0, rr * GRID_W:(rr + 1) * GRID_W, sl] = (
            jnp.where(low, o[:GRID_W], o[GRID_W:]).astype(BF16))

    pending = [scores(*u) for u in units[:SCORE_LOOKAHEAD]]
    for i, unit in enumerate(units):
        if i + SCORE_LOOKAHEAD < len(units):
            pending.append(scores(*units[i + SCORE_LOOKAHEAD]))
        finish(*unit, *pending.pop(0))


def _attn(q, k, v, kc, vc, bias, rows_per_step):
    b, s, width = q.shape
    n_rows = s // GRID_W
    c = kc.shape[1]
    tq = rows_per_step * GRID_W
    whole = lambda i, r: (i, 0, 0)
    return pl.pallas_call(
        functools.partial(_attn_kernel, rows_per_step=rows_per_step),
        grid=(b, n_rows // rows_per_step),
        in_specs=[pl.BlockSpec((1, tq, width), lambda i, r: (i, r, 0)),
                  pl.BlockSpec((1, s, width), whole),
                  pl.BlockSpec((1, s, width), whole),
                  pl.BlockSpec((1, c, width), whole),
                  pl.BlockSpec((1, c, width), whole),
                  pl.BlockSpec(bias.shape, lambda i, r: (0, 0, 0, 0),
                               pipeline_mode=pl.Buffered(1))],
        out_specs=pl.BlockSpec((1, tq, width), lambda i, r: (i, r, 0)),
        out_shape=jax.ShapeDtypeStruct((b, s, width), BF16),
        compiler_params=_params("arbitrary", "arbitrary"),
        name="attn",
    )(q, k, v, kc, vc, bias)


def _merge_kernel(x_ref, ya_ref, yf_ref, mod_ref, g1_ref, wg_ref, wfo_ref, wao_ref, wo_ref,
                  o_ref, *, chunk):
    d = g1_ref.shape[1]
    m = mod_ref[0]

    def branches(c):
        rows = slice(c * chunk, (c + 1) * chunk)
        h = _rms_mod(x_ref[0, rows, :], g1_ref[...], m[0:1], m[1:2]).astype(BF16)
        return (_dot(h, wg_ref[...]), _dot(yf_ref[0, rows, :], wfo_ref[...]),
                _dot(ya_ref[0, rows, :], wao_ref[...]))

    def finish(c, gate_logits, four, attn):
        rows = slice(c * chunk, (c + 1) * chunk)
        gates = jax.nn.sigmoid(gate_logits)
        merged = gates[:, :d] * four + gates[:, d:] * attn
        o_ref[0, rows, :] = x_ref[0, rows, :] + m[2:3] * _dot(merged.astype(BF16), wo_ref[...])

    n_chunks = x_ref.shape[1] // chunk
    pending = branches(0)
    for c in range(n_chunks):
        upcoming = branches(c + 1) if c + 1 < n_chunks else None
        finish(c, *pending)
        pending = upcoming


def _merge(x, y_attn, y_four, mod, g1, w_gate, w_fo, w_ao, w_o, tm):
    b, s, d = x.shape
    const = lambda i, t: (0, 0)
    tok = lambda i, t: (i, t, 0)
    return pl.pallas_call(
        functools.partial(_merge_kernel, chunk=256),
        grid=(b, s // tm),
        in_specs=[pl.BlockSpec((1, tm, d), tok),
                  pl.BlockSpec((1, tm, NA_WIDTH), tok),
                  pl.BlockSpec((1, tm, FOURIER_WIDTH), tok),
                  pl.BlockSpec((1, 8, d), lambda i, t: (i, 0, 0)),
                  pl.BlockSpec((1, d), const),
                  pl.BlockSpec(w_gate.shape, const),
                  pl.BlockSpec(w_fo.shape, const),
                  pl.BlockSpec(w_ao.shape, const),
                  pl.BlockSpec(w_o.shape, const)],
        out_specs=pl.BlockSpec((1, tm, d), tok),
        out_shape=jax.ShapeDtypeStruct(x.shape, F32),
        compiler_params=_params("arbitrary", "arbitrary"),
        name="merge",
    )(x, y_attn, y_four, mod, g1, w_gate, w_fo, w_ao, w_o)


def _mlp_kernel(x_ref, mod_ref, g2_ref, w1_ref, w2_ref, o_ref, *, ff_chunk):
    x = x_ref[0]
    m = mod_ref[0]
    h = _rms_mod(x, g2_ref[...], m[3:4], m[4:5]).astype(BF16)
    acc = jnp.zeros(x.shape, F32)
    for c in range(w1_ref.shape[1] // ff_chunk):
        a = jnp.maximum(_dot(h, w1_ref[:, c * ff_chunk:(c + 1) * ff_chunk]), 0.0)
        acc = acc + _dot((a * a).astype(BF16), w2_ref[c * ff_chunk:(c + 1) * ff_chunk, :])
    o_ref[0] = x + m[5:6] * acc


def _mlp(x, mod, g2, w1, w2, tm, ff_chunk):
    b, s, d = x.shape
    const = lambda i, t: (0, 0)
    return pl.pallas_call(
        functools.partial(_mlp_kernel, ff_chunk=ff_chunk),
        grid=(b, s // tm),
        in_specs=[pl.BlockSpec((1, tm, d), lambda i, t: (i, t, 0)),
                  pl.BlockSpec((1, 8, d), lambda i, t: (i, 0, 0)),
                  pl.BlockSpec((1, d), const),
                  pl.BlockSpec(w1.shape, const, pipeline_mode=pl.Buffered(1)),
                  pl.BlockSpec(w2.shape, const, pipeline_mode=pl.Buffered(1))],
        out_specs=pl.BlockSpec((1, tm, d), lambda i, t: (i, t, 0)),
        out_shape=jax.ShapeDtypeStruct(x.shape, F32),
        compiler_params=_params("arbitrary", "arbitrary"),
        name="mlp",
    )(x, mod, g2, w1, w2)


def _dft_tables(rows):
    n = rows * GRID_W
    gd = FOURIER_GROUP_DIM
    c = np.arange(gd)
    ang = 2.0 * np.pi * np.outer(c, c) / gd
    cs = np.concatenate([np.cos(ang), np.sin(ang)], axis=1) * 2.0 ** -3
    a = np.arange(rows)
    ang = 2.0 * np.pi * np.outer(a, a) / rows
    c1, s1 = np.cos(ang), np.sin(ang)
    w1 = np.block([[c1, -s1], [s1, c1]]) * 2.0 ** -4
    b = np.arange(GRID_W)
    k = a[:, None] + rows * b[None, :]
    ang = 2.0 * np.pi * k[:, :, None] * b[None, None, :] / n
    g2 = np.concatenate([np.cos(ang), -np.sin(ang)], axis=2) * 2.0 ** -3
    assert gd * rows * GRID_W == 2 ** 20
    return tuple(jnp.asarray(m, F32).astype(BF16) for m in (cs, w1, g2))


def _rope_tables(seq):
    t = np.arange(seq)
    axis_dim = HEAD_DIM // 2
    inv_freq = ROPE_THETA ** (-np.arange(0, axis_dim, 2, dtype=np.float64) / axis_dim)
    ang_r = (t // GRID_W)[:, None] * inv_freq
    ang_c = (t % GRID_W)[:, None] * inv_freq
    ang = np.concatenate([ang_r, ang_r, ang_c, ang_c], axis=1)
    sign = np.tile(np.repeat([-1.0, 1.0], axis_dim // 2), 2)
    reps = LANES // HEAD_DIM
    cos = np.tile(np.cos(ang), (1, reps))
    sin = np.tile(np.sin(ang) * sign, (1, reps))
    return jnp.asarray(cos, F32), jnp.asarray(sin, F32)


def _bias_table(rpb, n_rows):
    del n_rows
    c = np.arange(GRID_W)
    col_start = np.clip(c - WIN_COLS // 2, 0, GRID_W - WIN_COLS)
    kcol = np.arange(GRID_W)
    valid = (kcol[None, :] >= col_start[:, None]) & (kcol[None, :] < col_start[:, None] + WIN_COLS)
    period = 2 * GRID_W - 1
    left = GRID_W - WIN_COLS
    padded = jnp.pad(rpb, ((0, 0), (0, 0), (left, period - left - rpb.shape[-1])))
    flat = jnp.tile(padded, (1, 1, GRID_W + 1))[:, :, :GRID_W * (period + 1)]
    shifted = flat.reshape(rpb.shape[0], rpb.shape[1], GRID_W, period + 1)[..., :GRID_W]
    toep = shifted[:, :, ::-1, :].transpose(0, 2, 1, 3)
    toep = jnp.where(valid[None, :, None, :], toep * LOG2_E, MASK_VALUE)
    tab = jnp.stack([toep[:, :, WIN_ROWS - 1 - var:2 * WIN_ROWS - 1 - var]
                     for var in range(WIN_ROWS)], axis=0)
    heads_per_block = LANES // HEAD_DIM
    return tab.reshape(WIN_ROWS, NA_HEADS // heads_per_block, heads_per_block * GRID_W,
                       WIN_ROWS * GRID_W).astype(F32)


def kernel(x, c, ctx, c_ctx, w_ada, b_ada, norm1_g, norm2_g, w_in, q_norm_g, k_norm_g, rpb,
           w_branch_gate, w_fourier_out, w_attn_out, w_out, w_mlp1, w_mlp2):
    b, s, d = x.shape
    assert w_ada.shape[0] == 1, "single-layer block only"
    n_rows = s // GRID_W
    assert n_rows >= WIN_ROWS and b < 8

    cond = jnp.zeros((8, d), F32).at[:b].set(c).at[b].set(c_ctx)
    mods = _ada(cond, w_ada[0], b_ada)
    mod = jnp.pad(mods[:b].reshape(b, N_MOD, d), ((0, 0), (0, 8 - N_MOD), (0, 0)))
    mod_ctx = jnp.pad(mods[b].reshape(N_MOD, d), ((0, 8 - N_MOD), (0, 0)))

    half = NA_WIDTH // 2
    head_mean = jnp.asarray(
        np.kron(np.eye(half // HEAD_DIM), np.full((HEAD_DIM, HEAD_DIM), 1.0 / HEAD_DIM)),
        F32).astype(BF16)
    qg = jnp.tile(q_norm_g[0], NA_HEADS)[None, :] * (HEAD_DIM ** -0.5 * LOG2_E)
    kg = jnp.tile(k_norm_g[0], NA_HEADS)[None, :]
    cos, sin = _rope_tables(s)
    cs, w1, g2 = _dft_tables(n_rows)
    w_in_b = w_in[0].astype(BF16)

    kc, vc = _ctx_kv(ctx, mod_ctx, norm1_g, w_in_b, kg, head_mean)
    f, q, k, v = _proj(x, mod, norm1_g, w_in_b, qg, kg, head_mean, cos, sin, tm=1024)
    y_four = _fourier(f, cs, w1, g2)
    y_attn = _attn(q, k, v, kc, vc, _bias_table(rpb[0], n_rows), rows_per_step=4)
    x1 = _merge(x, y_attn, y_four, mod, norm1_g, w_branch_gate[0].astype(BF16),
                w_fourier_out[0].astype(BF16), w_attn_out[0].astype(BF16),
                w_out[0].astype(BF16), tm=1024)
    return _mlp(x1, mod, norm2_g, w_mlp1[0].astype(BF16), w_mlp2[0].astype(BF16),
                tm=1024, ff_chunk=1024)
```

```python
import functools

import numpy as np
import jax
import jax.numpy as jnp
from jax import lax
from jax.experimental import pallas as pl
from jax.experimental.pallas import tpu as pltpu

GRID_W = 64
N_FOURIER_GROUPS = 4
FOURIER_GROUP_DIM = 128
FOURIER_WIDTH = N_FOURIER_GROUPS * FOURIER_GROUP_DIM
NA_HEADS = 8
HEAD_DIM = 64
NA_WIDTH = NA_HEADS * HEAD_DIM
WIN_ROWS = 8
WIN_COLS = 16
ROPE_THETA = 10000.0
NORM_EPS = 1e-6
N_MOD = 6
MASK_VALUE = -1e30
LOG2_E = 1.4426950408889634
SCORE_LOOKAHEAD = 2
ROW_COLS = 4

LANES = 128
SUBLANES = 8
VMEM_LIMIT_BYTES = 48 * 1024 * 1024

BF16 = jnp.bfloat16
F32 = jnp.float32


def _dot(a, b):
    return jnp.dot(a, b, preferred_element_type=F32)


def _dot_nt(a, b):
    return lax.dot_general(a, b, (((1,), (1,)), ((), ())), preferred_element_type=F32)


def _params(*semantics):
    return pltpu.CompilerParams(dimension_semantics=semantics,
                                vmem_limit_bytes=VMEM_LIMIT_BYTES)


def _rms_mod(x, g, shift, scale):
    ms = jnp.mean(x * x, axis=-1, keepdims=True)
    return (x * lax.rsqrt(ms + NORM_EPS) * g) * (1.0 + scale) + shift


def _head_norm(a, gain, head_mean):
    half = NA_WIDTH // 2
    sq = (a * a).astype(BF16)
    ms = jnp.concatenate([_dot(sq[:, :half], head_mean), _dot(sq[:, half:], head_mean)], axis=1)
    return a * lax.rsqrt(ms + NORM_EPS) * gain


def _ada_kernel(cond_ref, w_ref, b_ref, o_ref):
    cond = cond_ref[...]
    s = cond * jax.nn.sigmoid(cond)
    s_hi = s.astype(BF16)
    s_lo = (s - s_hi.astype(F32)).astype(BF16)
    w = w_ref[...]
    w_hi = w.astype(BF16)
    w_lo = (w - w_hi.astype(F32)).astype(BF16)
    o_ref[...] = _dot(s_hi, w_hi) + _dot(s_hi, w_lo) + _dot(s_lo, w_hi) + b_ref[...]


def _ada(cond, w, b):
    rows, d = cond.shape
    n = w.shape[1]
    tn = 768
    return pl.pallas_call(
        _ada_kernel,
        grid=(n // tn,),
        in_specs=[pl.BlockSpec((rows, d), lambda j: (0, 0)),
                  pl.BlockSpec((d, tn), lambda j: (0, j)),
                  pl.BlockSpec((1, tn), lambda j: (0, j))],
        out_specs=pl.BlockSpec((rows, tn), lambda j: (0, j)),
        out_shape=jax.ShapeDtypeStruct((rows, n), F32),
        compiler_params=_params("arbitrary"),
        name="ada",
    )(cond, w, b)


def _ctx_kv_kernel(ctx_ref, mod_ref, g1_ref, w_ref, kg_ref, hm_ref, k_ref, v_ref):
    m = mod_ref[...]
    h = _rms_mod(ctx_ref[0], g1_ref[...], m[0:1], m[1:2])
    p = _dot(h.astype(BF16), w_ref[...])
    k_ref[0] = _head_norm(p[:, :NA_WIDTH], kg_ref[...], hm_ref[...]).astype(BF16)
    v_ref[0] = p[:, NA_WIDTH:].astype(BF16)


def _ctx_kv(ctx, mod_ctx, g1, w_in, kg, head_mean):
    b, c, d = ctx.shape
    kv_cols = 2 * NA_WIDTH
    out = jax.ShapeDtypeStruct((b, c, NA_WIDTH), BF16)
    return pl.pallas_call(
        _ctx_kv_kernel,
        grid=(b,),
        in_specs=[pl.BlockSpec((1, c, d), lambda i: (i, 0, 0)),
                  pl.BlockSpec((8, d), lambda i: (0, 0)),
                  pl.BlockSpec((1, d), lambda i: (0, 0)),
                  pl.BlockSpec((d, kv_cols), lambda i: (0, (w_in.shape[1] - kv_cols) // kv_cols)),
                  pl.BlockSpec((1, NA_WIDTH), lambda i: (0, 0)),
                  pl.BlockSpec(head_mean.shape, lambda i: (0, 0))],
        out_specs=[pl.BlockSpec((1, c, NA_WIDTH), lambda i: (i, 0, 0))] * 2,
        out_shape=[out, out],
        compiler_params=_params("arbitrary"),
        name="ctx_kv",
    )(ctx, mod_ctx, g1, w_in, kg, head_mean)


def _rope(a, cos, sin_signed, first_half):
    half = HEAD_DIM // 4
    partner = jnp.where(first_half, pltpu.roll(a, LANES - half, 1), pltpu.roll(a, half, 1))
    return a * cos + partner * sin_signed


def _proj_kernel(x_ref, mod_ref, g1_ref, w_ref, qg_ref, kg_ref, hm_ref, cos_ref, sin_ref,
                 f_ref, q_ref, k_ref, v_ref, *, chunk):
    m = mod_ref[0]
    hm = hm_ref[...]
    lane = lax.broadcasted_iota(jnp.int32, (chunk, LANES), 1)
    first_half = (lane % (HEAD_DIM // 2)) < (HEAD_DIM // 4)
    qk = ((FOURIER_WIDTH, qg_ref, q_ref), (FOURIER_WIDTH + NA_WIDTH, kg_ref, k_ref))

    def project(c):
        rows = slice(c * chunk, (c + 1) * chunk)
        h = _rms_mod(x_ref[0, rows, :], g1_ref[...], m[0:1], m[1:2])
        return _dot(h.astype(BF16), w_ref[...])

    def finish(c, p):
        rows = slice(c * chunk, (c + 1) * chunk)
        f_ref[0, rows, :] = p[:, :FOURIER_WIDTH].astype(BF16)
        v_ref[0, rows, :] = p[:, FOURIER_WIDTH + 2 * NA_WIDTH:].astype(BF16)
        cos = cos_ref[rows, :]
        sin = sin_ref[rows, :]
        for col, gain_ref, dst in qk:
            a = _head_norm(p[:, col:col + NA_WIDTH], gain_ref[...], hm)
            blocks = [_rope(a[:, j:j + LANES], cos, sin, first_half)
                      for j in range(0, NA_WIDTH, LANES)]
            dst[0, rows, :] = jnp.concatenate(blocks, axis=1).astype(BF16)

    n_chunks = x_ref.shape[1] // chunk
    pending = project(0)
    for c in range(n_chunks):
        upcoming = project(c + 1) if c + 1 < n_chunks else None
        finish(c, pending)
        pending = upcoming


def _proj(x, mod, g1, w_in, qg, kg, head_mean, cos, sin, tm):
    b, s, d = x.shape
    out = jax.ShapeDtypeStruct((b, s, NA_WIDTH), BF16)
    tok = lambda t, i: (i, t, 0)
    const = lambda t, i: (0, 0)
    return pl.pallas_call(
        functools.partial(_proj_kernel, chunk=128),
        grid=(s // tm, b),
        in_specs=[pl.BlockSpec((1, tm, d), tok),
                  pl.BlockSpec((1, 8, d), lambda t, i: (i, 0, 0)),
                  pl.BlockSpec((1, d), const),
                  pl.BlockSpec(w_in.shape, const),
                  pl.BlockSpec((1, NA_WIDTH), const),
                  pl.BlockSpec((1, NA_WIDTH), const),
                  pl.BlockSpec(head_mean.shape, const),
                  pl.BlockSpec((tm, cos.shape[1]), lambda t, i: (t, 0)),
                  pl.BlockSpec((tm, sin.shape[1]), lambda t, i: (t, 0))],
        out_specs=[pl.BlockSpec((1, tm, NA_WIDTH), tok)] * 4,
        out_shape=[out] * 4,
        compiler_params=_params("arbitrary", "arbitrary"),
        name="proj",
    )(x, mod, g1, w_in, qg, kg, head_mean, cos, sin)


def _pitch(n):
    tiles = -(-n // SUBLANES)
    return SUBLANES * (tiles + 1 - tiles % 2)


def _fourier_kernel(u_ref, cs_ref, w1_ref, g_ref, y_ref, ut_ref, z_ref, tr_ref, ti_ref, *,
                    rows, b_chunk):
    gd = FOURIER_GROUP_DIM
    w = GRID_W
    row_pitch = _pitch(rows)

    def reorder(a, carry):
        base = pl.multiple_of(a * w, w)
        ut_ref[pl.ds(a, w, stride=row_pitch), :] = u_ref[0, pl.ds(base, w), :].astype(F32)
        return carry

    def column(b):
        return ut_ref[pl.ds(pl.multiple_of(b * row_pitch, SUBLANES), rows), :].astype(BF16)

    def channel_stage(i, carry):
        b0 = i * b_chunk
        lhs = jnp.concatenate(
            [jnp.concatenate([column(b0 + 2 * p), column(b0 + 2 * p + 1)], axis=1)
             for p in range(b_chunk // 2)], axis=0)
        zc = _dot(lhs, cs_ref[...])
        for t in range(b_chunk):
            block = zc[(t // 2) * rows:(t // 2 + 1) * rows, (t % 2) * 2 * gd:(t % 2 + 1) * 2 * gd]
            z_ref[b0 + t, :rows] = block[:, :gd].astype(BF16)
            z_ref[b0 + t, rows:] = block[:, gd:].astype(BF16)
        return carry

    def row_stage(i, carry):
        t2 = _dot(w1_ref[...],
                  jnp.concatenate([z_ref[ROW_COLS * i + t] for t in range(ROW_COLS)], axis=1))
        for t in range(ROW_COLS):
            b = ROW_COLS * i + t
            for blk in range(rows // SUBLANES):
                dst = pl.multiple_of((blk * w + b) * SUBLANES, SUBLANES)
                lo = blk * SUBLANES
                tr_ref[pl.ds(dst, SUBLANES), :] = t2[lo:lo + SUBLANES, t * gd:(t + 1) * gd]
                ti_ref[pl.ds(dst, SUBLANES), :] = t2[rows + lo:rows + lo + SUBLANES,
                                                     t * gd:(t + 1) * gd]
        return carry

    def col_stage(blk, carry):
        for r_in in range(SUBLANES):
            ka = blk * SUBLANES + r_in
            start = blk * (w * SUBLANES) + r_in
            rhs = jnp.concatenate([tr_ref[pl.ds(start, w, stride=SUBLANES), :],
                                   ti_ref[pl.ds(start, w, stride=SUBLANES), :]], axis=0)
            y = _dot(g_ref[ka], rhs.astype(BF16))
            for kblk in range(w // SUBLANES):
                dst = pl.multiple_of((kblk * rows + ka) * SUBLANES, SUBLANES)
                ut_ref[pl.ds(dst, SUBLANES), :] = y[kblk * SUBLANES:(kblk + 1) * SUBLANES]
        return carry

    def emit(kblk, carry):
        for r_in in range(SUBLANES):
            kb = kblk * SUBLANES + r_in
            start = kblk * (rows * SUBLANES) + r_in
            dst = pl.multiple_of(kb * rows, rows)
            y_ref[0, pl.ds(dst, rows), :] = (
                ut_ref[pl.ds(start, rows, stride=SUBLANES), :].astype(BF16))
        return carry

    lax.fori_loop(0, rows, reorder, 0, unroll=8)
    for i in range(w // b_chunk):
        channel_stage(i, 0)
    for i in range(w // ROW_COLS):
        row_stage(i, 0)
    lax.fori_loop(0, rows // SUBLANES, col_stage, 0, unroll=2)
    lax.fori_loop(0, w // SUBLANES, emit, 0)


def _fourier(f, cs, w1, g2):
    b, s, width = f.shape
    rows = s // GRID_W
    gd = FOURIER_GROUP_DIM
    slab = pl.BlockSpec((1, s, gd), lambda i, g: (i, 0, g))
    const2 = lambda i, g: (0, 0)
    return pl.pallas_call(
        functools.partial(_fourier_kernel, rows=rows, b_chunk=8),
        grid=(b, width // gd),
        in_specs=[slab,
                  pl.BlockSpec(cs.shape, const2),
                  pl.BlockSpec(w1.shape, const2),
                  pl.BlockSpec(g2.shape, lambda i, g: (0, 0, 0))],
        out_specs=slab,
        out_shape=jax.ShapeDtypeStruct(f.shape, BF16),
        scratch_shapes=[pltpu.VMEM((GRID_W * _pitch(rows), gd), F32),
                        pltpu.VMEM((GRID_W, 2 * rows, gd), BF16),
                        pltpu.VMEM((rows * GRID_W, gd), F32),
                        pltpu.VMEM((rows * GRID_W, gd), F32)],
        compiler_params=_params("arbitrary", "arbitrary"),
        name="fourier",
    )(f, cs, w1, g2)


def _build_bias(rpb_ref, bias_ref):
    n_dr = 2 * WIN_ROWS - 1
    col = lax.broadcasted_iota(jnp.int32, (GRID_W, LANES), 0)
    lane = lax.broadcasted_iota(jnp.int32, (GRID_W, LANES), 1)
    kcol = lane % GRID_W
    col_start = jnp.clip(col - WIN_COLS // 2, 0, GRID_W - WIN_COLS)
    valid = (kcol >= col_start) & (kcol < col_start + WIN_COLS)
    low = lane < GRID_W

    def per_head(h, carry):
        pair = h // 2
        row0 = pl.multiple_of((h % 2) * GRID_W, GRID_W)
        at_low, at_high = [], []
        for dr in range(n_dr):
            p = jnp.broadcast_to(rpb_ref[pl.ds(h * n_dr + dr, 1), :], (GRID_W, LANES))
            at_low.append(pltpu.roll(p, LANES - GRID_W + 1, 1, stride=1, stride_axis=0))
            at_high.append(pltpu.roll(p, LANES - 2 * GRID_W + 1, 1, stride=1, stride_axis=0))
        for variant in range(WIN_ROWS):
            for jp in range(WIN_ROWS // 2):
                dr = 2 * jp - variant + WIN_ROWS - 1
                block = jnp.where(valid, jnp.where(low, at_low[dr], at_high[dr + 1]), MASK_VALUE)
                bias_ref[variant, pair, pl.ds(row0, GRID_W), jp * LANES:(jp + 1) * LANES] = block
        return carry

    lax.fori_loop(0, NA_HEADS, per_head, 0)


def _attn_kernel(q_ref, k_ref, v_ref, kc_ref, vc_ref, rpb_ref, o_ref, bias_ref, *,
                 rows_per_step):
    @pl.when((pl.program_id(0) == 0) & (pl.program_id(1) == 0))
    def _():
        _build_bias(rpb_ref, bias_ref)

    n_rows = k_ref.shape[1] // GRID_W
    n_loc = WIN_ROWS * GRID_W
    lane = lax.broadcasted_iota(jnp.int32, (GRID_W, LANES), 1)
    low = lane < HEAD_DIM
    units = [(rr, pair) for rr in range(rows_per_step) for pair in range(NA_WIDTH // LANES)]

    def window(rr):
        r = pl.program_id(1) * rows_per_step + rr
        row_start = jnp.clip(r - WIN_ROWS // 2, 0, n_rows - WIN_ROWS)
        return r - row_start, pl.multiple_of(row_start * GRID_W, GRID_W)

    def scores(rr, pair):
        variant, start = window(rr)
        sl = slice(pair * LANES, (pair + 1) * LANES)
        qp = q_ref[0, rr * GRID_W:(rr + 1) * GRID_W, sl]
        zero = jnp.zeros_like(qp)
        q2 = jnp.concatenate([jnp.where(low, qp, zero), jnp.where(low, zero, qp)], axis=0)
        s_loc = _dot_nt(q2, k_ref[0, pl.ds(start, n_loc), sl]) + bias_ref[variant, pair]
        s_ctx = _dot_nt(q2, kc_ref[0, :, sl])
        return s_loc, s_ctx

    def finish(rr, pair, s_loc, s_ctx):
        _, start = window(rr)
        sl = slice(pair * LANES, (pair + 1) * LANES)
        m = jnp.maximum(jnp.max(s_loc, axis=-1, keepdims=True),
                        jnp.max(s_ctx, axis=-1, keepdims=True))
        p_loc = jnp.exp2(s_loc - m)
        p_ctx = jnp.exp2(s_ctx - m)
        denom = (jnp.sum(p_loc, axis=-1, keepdims=True)
                 + jnp.sum(p_ctx, axis=-1, keepdims=True))
        o = (_dot(p_loc.astype(BF16), v_ref[0, pl.ds(start, n_loc), sl])
             + _dot(p_ctx.astype(BF16), vc_ref[0, :, sl]))
        o = o / denom
        o_ref[0, rr * GRID_W:(rr + 1) * GRID_W, sl] = (
            jnp.where(low, o[:GRID_W], o[GRID_W:]).astype(BF16))

    pending = [scores(*u) for u in units[:SCORE_LOOKAHEAD]]
    for i, unit in enumerate(units):
        if i + SCORE_LOOKAHEAD < len(units):
            pending.append(scores(*units[i + SCORE_LOOKAHEAD]))
        finish(*unit, *pending.pop(0))


def _attn(q, k, v, kc, vc, rpb_rows, rows_per_step):
    b, s, width = q.shape
    n_rows = s // GRID_W
    c = kc.shape[1]
    tq = rows_per_step * GRID_W
    whole = lambda i, r: (i, 0, 0)
    heads_per_block = LANES // HEAD_DIM
    bias_shape = (WIN_ROWS, NA_HEADS // heads_per_block, heads_per_block * GRID_W,
                  WIN_ROWS * GRID_W)
    return pl.pallas_call(
        functools.partial(_attn_kernel, rows_per_step=rows_per_step),
        grid=(b, n_rows // rows_per_step),
        in_specs=[pl.BlockSpec((1, tq, width), lambda i, r: (i, r, 0)),
                  pl.BlockSpec((1, s, width), whole),
                  pl.BlockSpec((1, s, width), whole),
                  pl.BlockSpec((1, c, width), whole),
                  pl.BlockSpec((1, c, width), whole),
                  pl.BlockSpec(rpb_rows.shape, lambda i, r: (0, 0))],
        out_specs=pl.BlockSpec((1, tq, width), lambda i, r: (i, r, 0)),
        out_shape=jax.ShapeDtypeStruct((b, s, width), BF16),
        scratch_shapes=[pltpu.VMEM(bias_shape, F32)],
        compiler_params=_params("arbitrary", "arbitrary"),
        name="attn",
    )(q, k, v, kc, vc, rpb_rows)


def _merge_kernel(x_ref, ya_ref, yf_ref, mod_ref, g1_ref, wg_ref, wfo_ref, wao_ref, wo_ref,
                  o_ref, *, chunk):
    d = g1_ref.shape[1]
    m = mod_ref[0]

    def branches(c):
        rows = slice(c * chunk, (c + 1) * chunk)
        h = _rms_mod(x_ref[0, rows, :], g1_ref[...], m[0:1], m[1:2]).astype(BF16)
        return (_dot(h, wg_ref[...]), _dot(yf_ref[0, rows, :], wfo_ref[...]),
                _dot(ya_ref[0, rows, :], wao_ref[...]))

    def finish(c, gate_logits, four, attn):
        rows = slice(c * chunk, (c + 1) * chunk)
        gates = jax.nn.sigmoid(gate_logits)
        merged = gates[:, :d] * four + gates[:, d:] * attn
        o_ref[0, rows, :] = x_ref[0, rows, :] + m[2:3] * _dot(merged.astype(BF16), wo_ref[...])

    n_chunks = x_ref.shape[1] // chunk
    pending = branches(0)
    for c in range(n_chunks):
        upcoming = branches(c + 1) if c + 1 < n_chunks else None
        finish(c, *pending)
        pending = upcoming


def _merge(x, y_attn, y_four, mod, g1, w_gate, w_fo, w_ao, w_o, tm):
    b, s, d = x.shape
    const = lambda i, t: (0, 0)
    tok = lambda i, t: (i, t, 0)
    return pl.pallas_call(
        functools.partial(_merge_kernel, chunk=256),
        grid=(b, s // tm),
        in_specs=[pl.BlockSpec((1, tm, d), tok),
                  pl.BlockSpec((1, tm, NA_WIDTH), tok),
                  pl.BlockSpec((1, tm, FOURIER_WIDTH), tok),
                  pl.BlockSpec((1, 8, d), lambda i, t: (i, 0, 0)),
                  pl.BlockSpec((1, d), const),
                  pl.BlockSpec(w_gate.shape, const),
                  pl.BlockSpec(w_fo.shape, const),
                  pl.BlockSpec(w_ao.shape, const),
                  pl.BlockSpec(w_o.shape, const)],
        out_specs=pl.BlockSpec((1, tm, d), tok),
        out_shape=jax.ShapeDtypeStruct(x.shape, F32),
        compiler_params=_params("arbitrary", "arbitrary"),
        name="merge",
    )(x, y_attn, y_four, mod, g1, w_gate, w_fo, w_ao, w_o)


def _mlp_kernel(x_ref, mod_ref, g2_ref, w1_ref, w2_ref, o_ref, *, ff_chunk):
    x = x_ref[0]
    m = mod_ref[0]
    h = _rms_mod(x, g2_ref[...], m[3:4], m[4:5]).astype(BF16)
    acc = jnp.zeros(x.shape, F32)
    for c in range(w1_ref.shape[1] // ff_chunk):
        a = jnp.maximum(_dot(h, w1_ref[:, c * ff_chunk:(c + 1) * ff_chunk]), 0.0)
        acc = acc + _dot((a * a).astype(BF16), w2_ref[c * ff_chunk:(c + 1) * ff_chunk, :])
    o_ref[0] = x + m[5:6] * acc


def _mlp(x, mod, g2, w1, w2, tm, ff_chunk):
    b, s, d = x.shape
    const = lambda i, t: (0, 0)
    return pl.pallas_call(
        functools.partial(_mlp_kernel, ff_chunk=ff_chunk),
        grid=(b, s // tm),
        in_specs=[pl.BlockSpec((1, tm, d), lambda i, t: (i, t, 0)),
                  pl.BlockSpec((1, 8, d), lambda i, t: (i, 0, 0)),
                  pl.BlockSpec((1, d), const),
                  pl.BlockSpec(w1.shape, const, pipeline_mode=pl.Buffered(1)),
                  pl.BlockSpec(w2.shape, const, pipeline_mode=pl.Buffered(1))],
        out_specs=pl.BlockSpec((1, tm, d), lambda i, t: (i, t, 0)),
        out_shape=jax.ShapeDtypeStruct(x.shape, F32),
        compiler_params=_params("arbitrary", "arbitrary"),
        name="mlp",
    )(x, mod, g2, w1, w2)


def _dft_tables(rows):
    n = rows * GRID_W
    gd = FOURIER_GROUP_DIM
    c = np.arange(gd)
    ang = 2.0 * np.pi * np.outer(c, c) / gd
    cs = np.concatenate([np.cos(ang), np.sin(ang)], axis=1) * 2.0 ** -3
    cs = np.kron(np.eye(2), cs)
    a = np.arange(rows)
    ang = 2.0 * np.pi * np.outer(a, a) / rows
    c1, s1 = np.cos(ang), np.sin(ang)
    w1 = np.block([[c1, -s1], [s1, c1]]) * 2.0 ** -4
    b = np.arange(GRID_W)
    k = a[:, None] + rows * b[None, :]
    ang = 2.0 * np.pi * k[:, :, None] * b[None, None, :] / n
    g2 = np.concatenate([np.cos(ang), -np.sin(ang)], axis=2) * 2.0 ** -3
    assert gd * rows * GRID_W == 2 ** 20
    return tuple(jnp.asarray(m, F32).astype(BF16) for m in (cs, w1, g2))


def _rope_tables(seq):
    t = np.arange(seq)
    axis_dim = HEAD_DIM // 2
    inv_freq = ROPE_THETA ** (-np.arange(0, axis_dim, 2, dtype=np.float64) / axis_dim)
    ang_r = (t // GRID_W)[:, None] * inv_freq
    ang_c = (t % GRID_W)[:, None] * inv_freq
    ang = np.concatenate([ang_r, ang_r, ang_c, ang_c], axis=1)
    sign = np.tile(np.repeat([-1.0, 1.0], axis_dim // 2), 2)
    reps = LANES // HEAD_DIM
    cos = np.tile(np.cos(ang), (1, reps))
    sin = np.tile(np.sin(ang) * sign, (1, reps))
    return jnp.asarray(cos, F32), jnp.asarray(sin, F32)


def _rpb_rows(rpb):
    left = GRID_W - WIN_COLS
    padded = jnp.pad(rpb * LOG2_E, ((0, 0), (0, 0), (left, LANES - left - rpb.shape[-1])))
    return padded.reshape(rpb.shape[0] * rpb.shape[1], LANES)


def kernel(x, c, ctx, c_ctx, w_ada, b_ada, norm1_g, norm2_g, w_in, q_norm_g, k_norm_g, rpb,
           w_branch_gate, w_fourier_out, w_attn_out, w_out, w_mlp1, w_mlp2):
    b, s, d = x.shape
    assert w_ada.shape[0] == 1, "single-layer block only"
    n_rows = s // GRID_W
    assert n_rows >= WIN_ROWS and b < 8

    cond = jnp.zeros((8, d), F32).at[:b].set(c).at[b].set(c_ctx)
    mods = _ada(cond, w_ada[0], b_ada)
    mod = jnp.pad(mods[:b].reshape(b, N_MOD, d), ((0, 0), (0, 8 - N_MOD), (0, 0)))
    mod_ctx = jnp.pad(mods[b].reshape(N_MOD, d), ((0, 8 - N_MOD), (0, 0)))

    half = NA_WIDTH // 2
    head_mean = jnp.asarray(
        np.kron(np.eye(half // HEAD_DIM), np.full((HEAD_DIM, HEAD_DIM), 1.0 / HEAD_DIM)),
        F32).astype(BF16)
    qg = jnp.tile(q_norm_g[0], NA_HEADS)[None, :] * (HEAD_DIM ** -0.5 * LOG2_E)
    kg = jnp.tile(k_norm_g[0], NA_HEADS)[None, :]
    cos, sin = _rope_tables(s)
    cs, w1, g2 = _dft_tables(n_rows)
    w_in_b = w_in[0].astype(BF16)

    kc, vc = _ctx_kv(ctx, mod_ctx, norm1_g, w_in_b, kg, head_mean)
    f, q, k, v = _proj(x, mod, norm1_g, w_in_b, qg, kg, head_mean, cos, sin, tm=1024)
    y_four = _fourier(f, cs, w1, g2)
    y_attn = _attn(q, k, v, kc, vc, _rpb_rows(rpb[0]), rows_per_step=4)
    x1 = _merge(x, y_attn, y_four, mod, norm1_g, w_branch_gate[0].astype(BF16),
                w_fourier_out[0].astype(BF16), w_attn_out[0].astype(BF16),
                w_out[0].astype(BF16), tm=1024)
    return _mlp(x1, mod, norm2_g, w_mlp1[0].astype(BF16), w_mlp2[0].astype(BF16),
                tm=1024, ff_chunk=1024)
```

```python
import functools

import numpy as np
import jax
import jax.numpy as jnp
from jax import lax
from jax.experimental import pallas as pl
from jax.experimental.pallas import tpu as pltpu

GRID_W = 64
N_FOURIER_GROUPS = 4
FOURIER_GROUP_DIM = 128
FOURIER_WIDTH = N_FOURIER_GROUPS * FOURIER_GROUP_DIM
NA_HEADS = 8
HEAD_DIM = 64
NA_WIDTH = NA_HEADS * HEAD_DIM
WIN_ROWS = 8
WIN_COLS = 16
ROPE_THETA = 10000.0
NORM_EPS = 1e-6
N_MOD = 6
MASK_VALUE = -1e30
LOG2_E = 1.4426950408889634
SCORE_LOOKAHEAD = 2
ROW_COLS = 4

LANES = 128
SUBLANES = 8
VMEM_LIMIT_BYTES = 48 * 1024 * 1024

BF16 = jnp.bfloat16
F32 = jnp.float32


def _dot(a, b):
    return jnp.dot(a, b, preferred_element_type=F32)


def _dot_nt(a, b):
    return lax.dot_general(a, b, (((1,), (1,)), ((), ())), preferred_element_type=F32)


def _params(*semantics):
    return pltpu.CompilerParams(dimension_semantics=semantics,
                                vmem_limit_bytes=VMEM_LIMIT_BYTES)


def _rms_mod(x, g, shift, scale):
    ms = jnp.mean(x * x, axis=-1, keepdims=True)
    return (x * lax.rsqrt(ms + NORM_EPS) * g) * (1.0 + scale) + shift


def _head_norm(a, gain, head_mean):
    half = NA_WIDTH // 2
    sq = (a * a).astype(BF16)
    ms = jnp.concatenate([_dot(sq[:, :half], head_mean), _dot(sq[:, half:], head_mean)], axis=1)
    return a * lax.rsqrt(ms + NORM_EPS) * gain


def _ada_kernel(cond_ref, w_ref, b_ref, o_ref):
    cond = cond_ref[...]
    s = cond * jax.nn.sigmoid(cond)
    s_hi = s.astype(BF16)
    s_lo = (s - s_hi.astype(F32)).astype(BF16)
    w = w_ref[...]
    w_hi = w.astype(BF16)
    w_lo = (w - w_hi.astype(F32)).astype(BF16)
    o_ref[...] = _dot(s_hi, w_hi) + _dot(s_hi, w_lo) + _dot(s_lo, w_hi) + b_ref[...]


def _ada(cond, w, b):
    rows, d = cond.shape
    n = w.shape[1]
    tn = 768
    return pl.pallas_call(
        _ada_kernel,
        grid=(n // tn,),
        in_specs=[pl.BlockSpec((rows, d), lambda j: (0, 0)),
                  pl.BlockSpec((d, tn), lambda j: (0, j)),
                  pl.BlockSpec((1, tn), lambda j: (0, j))],
        out_specs=pl.BlockSpec((rows, tn), lambda j: (0, j)),
        out_shape=jax.ShapeDtypeStruct((rows, n), F32),
        compiler_params=_params("arbitrary"),
        name="ada",
    )(cond, w, b)


def _ctx_kv_kernel(ctx_ref, mod_ref, g1_ref, w_ref, kg_ref, hm_ref, k_ref, v_ref):
    m = mod_ref[...]
    h = _rms_mod(ctx_ref[0], g1_ref[...], m[0:1], m[1:2])
    p = _dot(h.astype(BF16), w_ref[...])
    k_ref[0] = _head_norm(p[:, :NA_WIDTH], kg_ref[...], hm_ref[...]).astype(BF16)
    v_ref[0] = p[:, NA_WIDTH:].astype(BF16)


def _ctx_kv(ctx, mod_ctx, g1, w_in, kg, head_mean):
    b, c, d = ctx.shape
    kv_cols = 2 * NA_WIDTH
    out = jax.ShapeDtypeStruct((b, c, NA_WIDTH), BF16)
    return pl.pallas_call(
        _ctx_kv_kernel,
        grid=(b,),
        in_specs=[pl.BlockSpec((1, c, d), lambda i: (i, 0, 0)),
                  pl.BlockSpec((8, d), lambda i: (0, 0)),
                  pl.BlockSpec((1, d), lambda i: (0, 0)),
                  pl.BlockSpec((d, kv_cols), lambda i: (0, (w_in.shape[1] - kv_cols) // kv_cols)),
                  pl.BlockSpec((1, NA_WIDTH), lambda i: (0, 0)),
                  pl.BlockSpec(head_mean.shape, lambda i: (0, 0))],
        out_specs=[pl.BlockSpec((1, c, NA_WIDTH), lambda i: (i, 0, 0))] * 2,
        out_shape=[out, out],
        compiler_params=_params("arbitrary"),
        name="ctx_kv",
    )(ctx, mod_ctx, g1, w_in, kg, head_mean)


def _rope(a, cos, sin_signed, first_half):
    half = HEAD_DIM // 4
    partner = jnp.where(first_half, pltpu.roll(a, LANES - half, 1), pltpu.roll(a, half, 1))
    return a * cos + partner * sin_signed


def _proj_kernel(x_ref, mod_ref, g1_ref, w_ref, qg_ref, kg_ref, hm_ref, cos_ref, sin_ref,
                 f_ref, q_ref, k_ref, v_ref, *, chunk):
    m = mod_ref[0]
    hm = hm_ref[...]
    lane = lax.broadcasted_iota(jnp.int32, (chunk, LANES), 1)
    first_half = (lane % (HEAD_DIM // 2)) < (HEAD_DIM // 4)
    col_f, col_q, col_k, col_v = (0, FOURIER_WIDTH, FOURIER_WIDTH + NA_WIDTH,
                                  FOURIER_WIDTH + 2 * NA_WIDTH)

    def project(c):
        rows = slice(c * chunk, (c + 1) * chunk)
        h = _rms_mod(x_ref[0, rows, :], g1_ref[...], m[0:1], m[1:2]).astype(BF16)
        return tuple(_dot(h, w_ref[:, col:col + NA_WIDTH]) for col in (col_q, col_k, col_f, col_v))

    def finish(c, pq, pk, pf, pv):
        rows = slice(c * chunk, (c + 1) * chunk)
        cos = cos_ref[rows, :]
        sin = sin_ref[rows, :]
        for p, gain_ref, dst in ((pq, qg_ref, q_ref), (pk, kg_ref, k_ref)):
            a = _head_norm(p, gain_ref[...], hm)
            blocks = [_rope(a[:, j:j + LANES], cos, sin, first_half)
                      for j in range(0, NA_WIDTH, LANES)]
            dst[0, rows, :] = jnp.concatenate(blocks, axis=1).astype(BF16)
        f_ref[0, rows, :] = pf.astype(BF16)
        v_ref[0, rows, :] = pv.astype(BF16)

    n_chunks = x_ref.shape[1] // chunk
    pending = project(0)
    for c in range(n_chunks):
        upcoming = project(c + 1) if c + 1 < n_chunks else None
        finish(c, *pending)
        pending = upcoming


def _proj(x, mod, g1, w_in, qg, kg, head_mean, cos, sin, tm):
    b, s, d = x.shape
    out = jax.ShapeDtypeStruct((b, s, NA_WIDTH), BF16)
    tok = lambda t, i: (i, t, 0)
    const = lambda t, i: (0, 0)
    return pl.pallas_call(
        functools.partial(_proj_kernel, chunk=128),
        grid=(s // tm, b),
        in_specs=[pl.BlockSpec((1, tm, d), tok),
                  pl.BlockSpec((1, 8, d), lambda t, i: (i, 0, 0)),
                  pl.BlockSpec((1, d), const),
                  pl.BlockSpec(w_in.shape, const),
                  pl.BlockSpec((1, NA_WIDTH), const),
                  pl.BlockSpec((1, NA_WIDTH), const),
                  pl.BlockSpec(head_mean.shape, const),
                  pl.BlockSpec((tm, cos.shape[1]), lambda t, i: (t, 0)),
                  pl.BlockSpec((tm, sin.shape[1]), lambda t, i: (t, 0))],
        out_specs=[pl.BlockSpec((1, tm, NA_WIDTH), tok)] * 4,
        out_shape=[out] * 4,
        compiler_params=_params("arbitrary", "arbitrary"),
        name="proj",
    )(x, mod, g1, w_in, qg, kg, head_mean, cos, sin)


def _pitch(n):
    tiles = -(-n // SUBLANES)
    return SUBLANES * (tiles + 1 - tiles % 2)


def _fourier_kernel(u_ref, cs_ref, w1_ref, g_ref, y_ref, ut_ref, z_ref, tr_ref, ti_ref, *,
                    rows, b_chunk):
    gd = FOURIER_GROUP_DIM
    w = GRID_W
    row_pitch = _pitch(rows)

    def reorder(a, carry):
        base = pl.multiple_of(a * w, w)
        ut_ref[pl.ds(a, w, stride=row_pitch), :] = u_ref[0, pl.ds(base, w), :].astype(F32)
        return carry

    def column(b):
        return ut_ref[pl.ds(pl.multiple_of(b * row_pitch, SUBLANES), rows), :].astype(BF16)

    def channel_stage(i, carry):
        b0 = i * b_chunk
        lhs = jnp.concatenate(
            [jnp.concatenate([column(b0 + 2 * p), column(b0 + 2 * p + 1)], axis=1)
             for p in range(b_chunk // 2)], axis=0)
        zc = _dot(lhs, cs_ref[...])
        for t in range(b_chunk):
            block = zc[(t // 2) * rows:(t // 2 + 1) * rows, (t % 2) * 2 * gd:(t % 2 + 1) * 2 * gd]
            z_ref[b0 + t, :rows] = block[:, :gd].astype(BF16)
            z_ref[b0 + t, rows:] = block[:, gd:].astype(BF16)
        return carry

    def row_stage(i, carry):
        t2 = _dot(w1_ref[...],
                  jnp.concatenate([z_ref[ROW_COLS * i + t] for t in range(ROW_COLS)], axis=1))
        for t in range(ROW_COLS):
            b = ROW_COLS * i + t
            for blk in range(rows // SUBLANES):
                dst = pl.multiple_of((blk * w + b) * SUBLANES, SUBLANES)
                lo = blk * SUBLANES
                tr_ref[pl.ds(dst, SUBLANES), :] = t2[lo:lo + SUBLANES, t * gd:(t + 1) * gd]
                ti_ref[pl.ds(dst, SUBLANES), :] = t2[rows + lo:rows + lo + SUBLANES,
                                                     t * gd:(t + 1) * gd]
        return carry

    def col_stage(blk, carry):
        for r_in in range(SUBLANES):
            ka = blk * SUBLANES + r_in
            start = blk * (w * SUBLANES) + r_in
            rhs = jnp.concatenate([tr_ref[pl.ds(start, w, stride=SUBLANES), :],
                                   ti_ref[pl.ds(start, w, stride=SUBLANES), :]], axis=0)
            y = _dot(g_ref[ka], rhs.astype(BF16))
            for kblk in range(w // SUBLANES):
                dst = pl.multiple_of((kblk * rows + ka) * SUBLANES, SUBLANES)
                ut_ref[pl.ds(dst, SUBLANES), :] = y[kblk * SUBLANES:(kblk + 1) * SUBLANES]
        return carry

    def emit(kblk, carry):
        for r_in in range(SUBLANES):
            kb = kblk * SUBLANES + r_in
            start = kblk * (rows * SUBLANES) + r_in
            dst = pl.multiple_of(kb * rows, rows)
            y_ref[0, pl.ds(dst, rows), :] = (
                ut_ref[pl.ds(start, rows, stride=SUBLANES), :].astype(BF16))
        return carry

    lax.fori_loop(0, rows, reorder, 0, unroll=8)
    for i in range(w // b_chunk):
        channel_stage(i, 0)
    for i in range(w // ROW_COLS):
        row_stage(i, 0)
    lax.fori_loop(0, rows // SUBLANES, col_stage, 0, unroll=2)
    lax.fori_loop(0, w // SUBLANES, emit, 0)


def _fourier(f, cs, w1, g2):
    b, s, width = f.shape
    rows = s // GRID_W
    gd = FOURIER_GROUP_DIM
    slab = pl.BlockSpec((1, s, gd), lambda i, g: (i, 0, g))
    const2 = lambda i, g: (0, 0)
    return pl.pallas_call(
        functools.partial(_fourier_kernel, rows=rows, b_chunk=8),
        grid=(b, width // gd),
        in_specs=[slab,
                  pl.BlockSpec(cs.shape, const2),
                  pl.BlockSpec(w1.shape, const2),
                  pl.BlockSpec(g2.shape, lambda i, g: (0, 0, 0))],
        out_specs=slab,
        out_shape=jax.ShapeDtypeStruct(f.shape, BF16),
        scratch_shapes=[pltpu.VMEM((GRID_W * _pitch(rows), gd), F32),
                        pltpu.VMEM((GRID_W, 2 * rows, gd), BF16),
                        pltpu.VMEM((rows * GRID_W, gd), F32),
                        pltpu.VMEM((rows * GRID_W, gd), F32)],
        compiler_params=_params("arbitrary", "arbitrary"),
        name="fourier",
    )(f, cs, w1, g2)


def _build_bias(rpb_ref, bias_ref):
    n_dr = 2 * WIN_ROWS - 1
    col = lax.broadcasted_iota(jnp.int32, (GRID_W, LANES), 0)
    lane = lax.broadcasted_iota(jnp.int32, (GRID_W, LANES), 1)
    kcol = lane % GRID_W
    col_start = jnp.clip(col - WIN_COLS // 2, 0, GRID_W - WIN_COLS)
    valid = (kcol >= col_start) & (kcol < col_start + WIN_COLS)
    low = lane < GRID_W

    def per_head(h, carry):
        pair = h // 2
        row0 = pl.multiple_of((h % 2) * GRID_W, GRID_W)
        at_low, at_high = [], []
        for dr in range(n_dr):
            p = jnp.broadcast_to(rpb_ref[pl.ds(h * n_dr + dr, 1), :], (GRID_W, LANES))
            at_low.append(pltpu.roll(p, LANES - GRID_W + 1, 1, stride=1, stride_axis=0))
            at_high.append(pltpu.roll(p, LANES - 2 * GRID_W + 1, 1, stride=1, stride_axis=0))
        for variant in range(WIN_ROWS):
            for jp in range(WIN_ROWS // 2):
                dr = 2 * jp - variant + WIN_ROWS - 1
                block = jnp.where(valid, jnp.where(low, at_low[dr], at_high[dr + 1]), MASK_VALUE)
                bias_ref[variant, pair, pl.ds(row0, GRID_W), jp * LANES:(jp + 1) * LANES] = block
        return carry

    lax.fori_loop(0, NA_HEADS, per_head, 0)


def _attn_kernel(q_ref, k_ref, v_ref, kc_ref, vc_ref, rpb_ref, o_ref, bias_ref, *,
                 rows_per_step):
    @pl.when((pl.program_id(0) == 0) & (pl.program_id(1) == 0))
    def _():
        _build_bias(rpb_ref, bias_ref)

    n_rows = k_ref.shape[1] // GRID_W
    n_loc = WIN_ROWS * GRID_W
    lane = lax.broadcasted_iota(jnp.int32, (GRID_W, LANES), 1)
    low = lane < HEAD_DIM
    units = [(rr, pair) for rr in range(rows_per_step) for pair in range(NA_WIDTH // LANES)]

    def window(rr):
        r = pl.program_id(1) * rows_per_step + rr
        row_start = jnp.clip(r - WIN_ROWS // 2, 0, n_rows - WIN_ROWS)
        return r - row_start, pl.multiple_of(row_start * GRID_W, GRID_W)

    def scores(rr, pair):
        variant, start = window(rr)
        sl = slice(pair * LANES, (pair + 1) * LANES)
        qp = q_ref[0, rr * GRID_W:(rr + 1) * GRID_W, sl]
        zero = jnp.zeros_like(qp)
        q2 = jnp.concatenate([jnp.where(low, qp, zero), jnp.where(low, zero, qp)], axis=0)
        s_loc = _dot_nt(q2, k_ref[0, pl.ds(start, n_loc), sl]) + bias_ref[variant, pair]
        s_ctx = _dot_nt(q2, kc_ref[0, :, sl])
        return s_loc, s_ctx

    def finish(rr, pair, s_loc, s_ctx):
        _, start = window(rr)
        sl = slice(pair * LANES, (pair + 1) * LANES)
        m = jnp.maximum(jnp.max(s_loc, axis=-1, keepdims=True),
                        jnp.max(s_ctx, axis=-1, keepdims=True))
        p_loc = jnp.exp2(s_loc - m)
        p_ctx = jnp.exp2(s_ctx - m)
        denom = (jnp.sum(p_loc, axis=-1, keepdims=True)
                 + jnp.sum(p_ctx, axis=-1, keepdims=True))
        o = (_dot(p_loc.astype(BF16), v_ref[0, pl.ds(start, n_loc), sl])
             + _dot(p_ctx.astype(BF16), vc_ref[0, :, sl]))
        o = o / denom
        o_ref[0, rr * GRID_W:(rr + 1) * GRID_W, sl] = (
            jnp.where(low, o[:GRID_W], o[GRID_W:]).astype(BF16))

    pending = [scores(*u) for u in units[:SCORE_LOOKAHEAD]]
    for i, unit in enumerate(units):
        if i + SCORE_LOOKAHEAD < len(units):
            pending.append(scores(*units[i + SCORE_LOOKAHEAD]))
        finish(*unit, *pending.pop(0))


def _attn(q, k, v, kc, vc, rpb_rows, rows_per_step):
    b, s, width = q.shape
    n_rows = s // GRID_W
    c = kc.shape[1]
    tq = rows_per_step * GRID_W
    whole = lambda i, r: (i, 0, 0)
    heads_per_block = LANES // HEAD_DIM
    bias_shape = (WIN_ROWS, NA_HEADS // heads_per_block, heads_per_block * GRID_W,
                  WIN_ROWS * GRID_W)
    return pl.pallas_call(
        functools.partial(_attn_kernel, rows_per_step=rows_per_step),
        grid=(b, n_rows // rows_per_step),
        in_specs=[pl.BlockSpec((1, tq, width), lambda i, r: (i, r, 0)),
                  pl.BlockSpec((1, s, width), whole),
                  pl.BlockSpec((1, s, width), whole),
                  pl.BlockSpec((1, c, width), whole),
                  pl.BlockSpec((1, c, width), whole),
                  pl.BlockSpec(rpb_rows.shape, lambda i, r: (0, 0))],
        out_specs=pl.BlockSpec((1, tq, width), lambda i, r: (i, r, 0)),
        out_shape=jax.ShapeDtypeStruct((b, s, width), BF16),
        scratch_shapes=[pltpu.VMEM(bias_shape, F32)],
        compiler_params=_params("arbitrary", "arbitrary"),
        name="attn",
    )(q, k, v, kc, vc, rpb_rows)


def _merge_kernel(x_ref, ya_ref, yf_ref, mod_ref, g1_ref, wg_ref, wfo_ref, wao_ref, wo_ref,
                  o_ref, *, chunk):
    d = g1_ref.shape[1]
    m = mod_ref[0]

    def branches(c):
        rows = slice(c * chunk, (c + 1) * chunk)
        h = _rms_mod(x_ref[0, rows, :], g1_ref[...], m[0:1], m[1:2]).astype(BF16)
        return (_dot(h, wg_ref[...]), _dot(yf_ref[0, rows, :], wfo_ref[...]),
                _dot(ya_ref[0, rows, :], wao_ref[...]))

    def finish(c, gate_logits, four, attn):
        rows = slice(c * chunk, (c + 1) * chunk)
        gates = jax.nn.sigmoid(gate_logits)
        merged = gates[:, :d] * four + gates[:, d:] * attn
        o_ref[0, rows, :] = x_ref[0, rows, :] + m[2:3] * _dot(merged.astype(BF16), wo_ref[...])

    n_chunks = x_ref.shape[1] // chunk
    pending = branches(0)
    for c in range(n_chunks):
        upcoming = branches(c + 1) if c + 1 < n_chunks else None
        finish(c, *pending)
        pending = upcoming


def _merge(x, y_attn, y_four, mod, g1, w_gate, w_fo, w_ao, w_o, tm):
    b, s, d = x.shape
    const = lambda i, t: (0, 0)
    tok = lambda i, t: (i, t, 0)
    return pl.pallas_call(
        functools.partial(_merge_kernel, chunk=256),
        grid=(b, s // tm),
        in_specs=[pl.BlockSpec((1, tm, d), tok),
                  pl.BlockSpec((1, tm, NA_WIDTH), tok),
                  pl.BlockSpec((1, tm, FOURIER_WIDTH), tok),
                  pl.BlockSpec((1, 8, d), lambda i, t: (i, 0, 0)),
                  pl.BlockSpec((1, d), const),
                  pl.BlockSpec(w_gate.shape, const),
                  pl.BlockSpec(w_fo.shape, const),
                  pl.BlockSpec(w_ao.shape, const),
                  pl.BlockSpec(w_o.shape, const)],
        out_specs=pl.BlockSpec((1, tm, d), tok),
        out_shape=jax.ShapeDtypeStruct(x.shape, F32),
        compiler_params=_params("arbitrary", "arbitrary"),
        name="merge",
    )(x, y_attn, y_four, mod, g1, w_gate, w_fo, w_ao, w_o)


def _mlp_kernel(x_ref, mod_ref, g2_ref, w1_ref, w2_ref, o_ref, *, ff_chunk):
    x = x_ref[0]
    m = mod_ref[0]
    h = _rms_mod(x, g2_ref[...], m[3:4], m[4:5]).astype(BF16)
    acc = jnp.zeros(x.shape, F32)
    for c in range(w1_ref.shape[1] // ff_chunk):
        a = jnp.maximum(_dot(h, w1_ref[:, c * ff_chunk:(c + 1) * ff_chunk]), 0.0)
        acc = acc + _dot((a * a).astype(BF16), w2_ref[c * ff_chunk:(c + 1) * ff_chunk, :])
    o_ref[0] = x + m[5:6] * acc


def _mlp(x, mod, g2, w1, w2, tm, ff_chunk):
    b, s, d = x.shape
    const = lambda i, t: (0, 0)
    return pl.pallas_call(
        functools.partial(_mlp_kernel, ff_chunk=ff_chunk),
        grid=(b, s // tm),
        in_specs=[pl.BlockSpec((1, tm, d), lambda i, t: (i, t, 0)),
                  pl.BlockSpec((1, 8, d), lambda i, t: (i, 0, 0)),
                  pl.BlockSpec((1, d), const),
                  pl.BlockSpec(w1.shape, const, pipeline_mode=pl.Buffered(1)),
                  pl.BlockSpec(w2.shape, const, pipeline_mode=pl.Buffered(1))],
        out_specs=pl.BlockSpec((1, tm, d), lambda i, t: (i, t, 0)),
        out_shape=jax.ShapeDtypeStruct(x.shape, F32),
        compiler_params=_params("arbitrary", "arbitrary"),
        name="mlp",
    )(x, mod, g2, w1, w2)


def _dft_tables(rows):
    n = rows * GRID_W
    gd = FOURIER_GROUP_DIM
    c = np.arange(gd)
    ang = 2.0 * np.pi * np.outer(c, c) / gd
    cs = np.concatenate([np.cos(ang), np.sin(ang)], axis=1) * 2.0 ** -3
    cs = np.kron(np.eye(2), cs)
    a = np.arange(rows)
    ang = 2.0 * np.pi * np.outer(a, a) / rows
    c1, s1 = np.cos(ang), np.sin(ang)
    w1 = np.block([[c1, -s1], [s1, c1]]) * 2.0 ** -4
    b = np.arange(GRID_W)
    k = a[:, None] + rows * b[None, :]
    ang = 2.0 * np.pi * k[:, :, None] * b[None, None, :] / n
    g2 = np.concatenate([np.cos(ang), -np.sin(ang)], axis=2) * 2.0 ** -3
    assert gd * rows * GRID_W == 2 ** 20
    return tuple(jnp.asarray(m, F32).astype(BF16) for m in (cs, w1, g2))


def _rope_tables(seq):
    t = np.arange(seq)
    axis_dim = HEAD_DIM // 2
    inv_freq = ROPE_THETA ** (-np.arange(0, axis_dim, 2, dtype=np.float64) / axis_dim)
    ang_r = (t // GRID_W)[:, None] * inv_freq
    ang_c = (t % GRID_W)[:, None] * inv_freq
    ang = np.concatenate([ang_r, ang_r, ang_c, ang_c], axis=1)
    sign = np.tile(np.repeat([-1.0, 1.0], axis_dim // 2), 2)
    reps = LANES // HEAD_DIM
    cos = np.tile(np.cos(ang), (1, reps))
    sin = np.tile(np.sin(ang) * sign, (1, reps))
    return jnp.asarray(cos, F32), jnp.asarray(sin, F32)


def _rpb_rows(rpb):
    left = GRID_W - WIN_COLS
    padded = jnp.pad(rpb * LOG2_E, ((0, 0), (0, 0), (left, LANES - left - rpb.shape[-1])))
    return padded.reshape(rpb.shape[0] * rpb.shape[1], LANES)


def kernel(x, c, ctx, c_ctx, w_ada, b_ada, norm1_g, norm2_g, w_in, q_norm_g, k_norm_g, rpb,
           w_branch_gate, w_fourier_out, w_attn_out, w_out, w_mlp1, w_mlp2):
    b, s, d = x.shape
    assert w_ada.shape[0] == 1, "single-layer block only"
    n_rows = s // GRID_W
    assert n_rows >= WIN_ROWS and b < 8

    cond = jnp.zeros((8, d), F32).at[:b].set(c).at[b].set(c_ctx)
    mods = _ada(cond, w_ada[0], b_ada)
    mod = jnp.pad(mods[:b].reshape(b, N_MOD, d), ((0, 0), (0, 8 - N_MOD), (0, 0)))
    mod_ctx = jnp.pad(mods[b].reshape(N_MOD, d), ((0, 8 - N_MOD), (0, 0)))

    half = NA_WIDTH // 2
    head_mean = jnp.asarray(
        np.kron(np.eye(half // HEAD_DIM), np.full((HEAD_DIM, HEAD_DIM), 1.0 / HEAD_DIM)),
        F32).astype(BF16)
    qg = jnp.tile(q_norm_g[0], NA_HEADS)[None, :] * (HEAD_DIM ** -0.5 * LOG2_E)
    kg = jnp.tile(k_norm_g[0], NA_HEADS)[None, :]
    cos, sin = _rope_tables(s)
    cs, w1, g2 = _dft_tables(n_rows)
    w_in_b = w_in[0].astype(BF16)

    kc, vc = _ctx_kv(ctx, mod_ctx, norm1_g, w_in_b, kg, head_mean)
    f, q, k, v = _proj(x, mod, norm1_g, w_in_b, qg, kg, head_mean, cos, sin, tm=1024)
    y_four = _fourier(f, cs, w1, g2)
    y_attn = _attn(q, k, v, kc, vc, _rpb_rows(rpb[0]), rows_per_step=8)
    x1 = _merge(x, y_attn, y_four, mod, norm1_g, w_branch_gate[0].astype(BF16),
                w_fourier_out[0].astype(BF16), w_attn_out[0].astype(BF16),
                w_out[0].astype(BF16), tm=1024)
    return _mlp(x1, mod, norm2_g, w_mlp1[0].astype(BF16), w_mlp2[0].astype(BF16),
                tm=1024, ff_chunk=1024)
```

```python
import functools

import numpy as np
import jax
import jax.numpy as jnp
from jax import lax
from jax.experimental import pallas as pl
from jax.experimental.pallas import tpu as pltpu

GRID_W = 64
N_FOURIER_GROUPS = 4
FOURIER_GROUP_DIM = 128
FOURIER_WIDTH = N_FOURIER_GROUPS * FOURIER_GROUP_DIM
NA_HEADS = 8
HEAD_DIM = 64
NA_WIDTH = NA_HEADS * HEAD_DIM
WIN_ROWS = 8
WIN_COLS = 16
ROPE_THETA = 10000.0
NORM_EPS = 1e-6
N_MOD = 6
MASK_VALUE = -1e30
LOG2_E = 1.4426950408889634
SCORE_LOOKAHEAD = 2
ROW_COLS = 4

LANES = 128
SUBLANES = 8
VMEM_LIMIT_BYTES = 48 * 1024 * 1024

BF16 = jnp.bfloat16
F32 = jnp.float32


def _dot(a, b):
    return jnp.dot(a, b, preferred_element_type=F32)


def _dot_nt(a, b):
    return lax.dot_general(a, b, (((1,), (1,)), ((), ())), preferred_element_type=F32)


def _params(*semantics):
    return pltpu.CompilerParams(dimension_semantics=semantics,
                                vmem_limit_bytes=VMEM_LIMIT_BYTES)


def _rms_mod(x, g, shift, scale):
    ms = jnp.mean(x * x, axis=-1, keepdims=True)
    return (x * lax.rsqrt(ms + NORM_EPS) * g) * (1.0 + scale) + shift


def _head_norm(a, gain, head_mean):
    half = NA_WIDTH // 2
    sq = (a * a).astype(BF16)
    ms = jnp.concatenate([_dot(sq[:, :half], head_mean), _dot(sq[:, half:], head_mean)], axis=1)
    return a * lax.rsqrt(ms + NORM_EPS) * gain


def _ada_kernel(cond_ref, w_ref, b_ref, o_ref):
    cond = cond_ref[...]
    s = cond * jax.nn.sigmoid(cond)
    s_hi = s.astype(BF16)
    s_lo = (s - s_hi.astype(F32)).astype(BF16)
    w = w_ref[...]
    w_hi = w.astype(BF16)
    w_lo = (w - w_hi.astype(F32)).astype(BF16)
    o_ref[...] = _dot(s_hi, w_hi) + _dot(s_hi, w_lo) + _dot(s_lo, w_hi) + b_ref[...]


def _ada(cond, w, b):
    rows, d = cond.shape
    n = w.shape[1]
    tn = 768
    return pl.pallas_call(
        _ada_kernel,
        grid=(n // tn,),
        in_specs=[pl.BlockSpec((rows, d), lambda j: (0, 0)),
                  pl.BlockSpec((d, tn), lambda j: (0, j)),
                  pl.BlockSpec((1, tn), lambda j: (0, j))],
        out_specs=pl.BlockSpec((rows, tn), lambda j: (0, j)),
        out_shape=jax.ShapeDtypeStruct((rows, n), F32),
        compiler_params=_params("arbitrary"),
        name="ada",
    )(cond, w, b)


def _ctx_kv_kernel(ctx_ref, mod_ref, g1_ref, w_ref, kg_ref, hm_ref, k_ref, v_ref):
    m = mod_ref[...]
    h = _rms_mod(ctx_ref[0], g1_ref[...], m[0:1], m[1:2])
    p = _dot(h.astype(BF16), w_ref[...])
    k_ref[0] = _head_norm(p[:, :NA_WIDTH], kg_ref[...], hm_ref[...]).astype(BF16)
    v_ref[0] = p[:, NA_WIDTH:].astype(BF16)


def _ctx_kv(ctx, mod_ctx, g1, w_in, kg, head_mean):
    b, c, d = ctx.shape
    kv_cols = 2 * NA_WIDTH
    out = jax.ShapeDtypeStruct((b, c, NA_WIDTH), BF16)
    return pl.pallas_call(
        _ctx_kv_kernel,
        grid=(b,),
        in_specs=[pl.BlockSpec((1, c, d), lambda i: (i, 0, 0)),
                  pl.BlockSpec((8, d), lambda i: (0, 0)),
                  pl.BlockSpec((1, d), lambda i: (0, 0)),
                  pl.BlockSpec((d, kv_cols), lambda i: (0, (w_in.shape[1] - kv_cols) // kv_cols)),
                  pl.BlockSpec((1, NA_WIDTH), lambda i: (0, 0)),
                  pl.BlockSpec(head_mean.shape, lambda i: (0, 0))],
        out_specs=[pl.BlockSpec((1, c, NA_WIDTH), lambda i: (i, 0, 0))] * 2,
        out_shape=[out, out],
        compiler_params=_params("arbitrary"),
        name="ctx_kv",
    )(ctx, mod_ctx, g1, w_in, kg, head_mean)


def _rope(a, cos, sin_signed, first_half):
    half = HEAD_DIM // 4
    partner = jnp.where(first_half, pltpu.roll(a, LANES - half, 1), pltpu.roll(a, half, 1))
    return a * cos + partner * sin_signed


def _proj_kernel(x_ref, mod_ref, g1_ref, w_ref, qg_ref, kg_ref, hm_ref, cos_ref, sin_ref,
                 f_ref, q_ref, k_ref, v_ref, *, chunk):
    m = mod_ref[0]
    hm = hm_ref[...]
    lane = lax.broadcasted_iota(jnp.int32, (chunk, LANES), 1)
    first_half = (lane % (HEAD_DIM // 2)) < (HEAD_DIM // 4)
    col_f, col_q, col_k, col_v = (0, FOURIER_WIDTH, FOURIER_WIDTH + NA_WIDTH,
                                  FOURIER_WIDTH + 2 * NA_WIDTH)

    def project(c):
        rows = slice(c * chunk, (c + 1) * chunk)
        h = _rms_mod(x_ref[0, rows, :], g1_ref[...], m[0:1], m[1:2]).astype(BF16)
        return tuple(_dot(h, w_ref[:, col:col + NA_WIDTH]) for col in (col_q, col_k, col_f, col_v))

    def finish(c, pq, pk, pf, pv):
        rows = slice(c * chunk, (c + 1) * chunk)
        cos = cos_ref[rows, :]
        sin = sin_ref[rows, :]
        for p, gain_ref, dst in ((pq, qg_ref, q_ref), (pk, kg_ref, k_ref)):
            a = _head_norm(p, gain_ref[...], hm)
            blocks = [_rope(a[:, j:j + LANES], cos, sin, first_half)
                      for j in range(0, NA_WIDTH, LANES)]
            dst[0, rows, :] = jnp.concatenate(blocks, axis=1).astype(BF16)
        f_ref[0, rows, :] = pf.astype(BF16)
        v_ref[0, rows, :] = pv.astype(BF16)

    n_chunks = x_ref.shape[1] // chunk
    pending = project(0)
    for c in range(n_chunks):
        upcoming = project(c + 1) if c + 1 < n_chunks else None
        finish(c, *pending)
        pending = upcoming


def _proj(x, mod, g1, w_in, qg, kg, head_mean, cos, sin, tm):
    b, s, d = x.shape
    out = jax.ShapeDtypeStruct((b, s, NA_WIDTH), BF16)
    tok = lambda t, i: (i, t, 0)
    const = lambda t, i: (0, 0)
    return pl.pallas_call(
        functools.partial(_proj_kernel, chunk=128),
        grid=(s // tm, b),
        in_specs=[pl.BlockSpec((1, tm, d), tok),
                  pl.BlockSpec((1, 8, d), lambda t, i: (i, 0, 0)),
                  pl.BlockSpec((1, d), const),
                  pl.BlockSpec(w_in.shape, const),
                  pl.BlockSpec((1, NA_WIDTH), const),
                  pl.BlockSpec((1, NA_WIDTH), const),
                  pl.BlockSpec(head_mean.shape, const),
                  pl.BlockSpec((tm, cos.shape[1]), lambda t, i: (t, 0)),
                  pl.BlockSpec((tm, sin.shape[1]), lambda t, i: (t, 0))],
        out_specs=[pl.BlockSpec((1, tm, NA_WIDTH), tok)] * 4,
        out_shape=[out] * 4,
        compiler_params=_params("arbitrary", "arbitrary"),
        name="proj",
    )(x, mod, g1, w_in, qg, kg, head_mean, cos, sin)


def _pitch(n):
    tiles = -(-n // SUBLANES)
    return SUBLANES * (tiles + 1 - tiles % 2)


def _fourier_kernel(*refs, rows, b_chunk, n_cast):
    u_ref, cs_ref, w1_ref, g_ref = refs[:4]
    cast_src = refs[4:4 + n_cast]
    y_ref = refs[4 + n_cast]
    cast_dst = refs[5 + n_cast:5 + 2 * n_cast]
    ut_ref, z_ref, tr_ref, ti_ref = refs[5 + 2 * n_cast:]
    gd = FOURIER_GROUP_DIM
    w = GRID_W
    row_pitch = _pitch(rows)

    def reorder(a, carry):
        base = pl.multiple_of(a * w, w)
        ut_ref[pl.ds(a, w, stride=row_pitch), :] = u_ref[0, pl.ds(base, w), :].astype(F32)
        return carry

    def column(b):
        return ut_ref[pl.ds(pl.multiple_of(b * row_pitch, SUBLANES), rows), :].astype(BF16)

    def channel_stage(i, carry):
        b0 = i * b_chunk
        lhs = jnp.concatenate(
            [jnp.concatenate([column(b0 + 2 * p), column(b0 + 2 * p + 1)], axis=1)
             for p in range(b_chunk // 2)], axis=0)
        zc = _dot(lhs, cs_ref[...])
        for t in range(b_chunk):
            block = zc[(t // 2) * rows:(t // 2 + 1) * rows, (t % 2) * 2 * gd:(t % 2 + 1) * 2 * gd]
            z_ref[b0 + t, :rows] = block[:, :gd].astype(BF16)
            z_ref[b0 + t, rows:] = block[:, gd:].astype(BF16)
        return carry

    def row_stage(i, carry):
        t2 = _dot(w1_ref[...],
                  jnp.concatenate([z_ref[ROW_COLS * i + t] for t in range(ROW_COLS)], axis=1))
        for t in range(ROW_COLS):
            b = ROW_COLS * i + t
            for blk in range(rows // SUBLANES):
                dst = pl.multiple_of((blk * w + b) * SUBLANES, SUBLANES)
                lo = blk * SUBLANES
                tr_ref[pl.ds(dst, SUBLANES), :] = t2[lo:lo + SUBLANES, t * gd:(t + 1) * gd]
                ti_ref[pl.ds(dst, SUBLANES), :] = t2[rows + lo:rows + lo + SUBLANES,
                                                     t * gd:(t + 1) * gd]
        return carry

    def col_stage(blk, carry):
        for r_in in range(SUBLANES):
            ka = blk * SUBLANES + r_in
            start = blk * (w * SUBLANES) + r_in
            rhs = jnp.concatenate([tr_ref[pl.ds(start, w, stride=SUBLANES), :],
                                   ti_ref[pl.ds(start, w, stride=SUBLANES), :]], axis=0)
            y = _dot(g_ref[ka], rhs.astype(BF16))
            for kblk in range(w // SUBLANES):
                dst = pl.multiple_of((kblk * rows + ka) * SUBLANES, SUBLANES)
                ut_ref[pl.ds(dst, SUBLANES), :] = y[kblk * SUBLANES:(kblk + 1) * SUBLANES]
        return carry

    def emit(kblk, carry):
        for r_in in range(SUBLANES):
            kb = kblk * SUBLANES + r_in
            start = kblk * (rows * SUBLANES) + r_in
            dst = pl.multiple_of(kb * rows, rows)
            y_ref[0, pl.ds(dst, rows), :] = (
                ut_ref[pl.ds(start, rows, stride=SUBLANES), :].astype(BF16))
        return carry

    lax.fori_loop(0, rows, reorder, 0, unroll=8)
    for src, dst in zip(cast_src, cast_dst):
        dst[...] = src[...].astype(BF16)
    for i in range(w // b_chunk):
        channel_stage(i, 0)
    for i in range(w // ROW_COLS):
        row_stage(i, 0)
    lax.fori_loop(0, rows // SUBLANES, col_stage, 0, unroll=2)
    lax.fori_loop(0, w // SUBLANES, emit, 0)


def _fourier(f, cs, w1, g2, weights):
    b, s, width = f.shape
    rows = s // GRID_W
    gd = FOURIER_GROUP_DIM
    n_groups = width // gd
    n_steps = b * n_groups
    slab = pl.BlockSpec((1, s, gd), lambda i, g: (i, 0, g))
    const2 = lambda i, g: (0, 0)
    slices = [pl.BlockSpec((wt.shape[0] // n_steps, wt.shape[1]),
                           lambda i, g: (i * n_groups + g, 0)) for wt in weights]
    assert all(wt.shape[0] % (n_steps * 2 * SUBLANES) == 0 for wt in weights)
    out = pl.pallas_call(
        functools.partial(_fourier_kernel, rows=rows, b_chunk=8, n_cast=len(weights)),
        grid=(b, n_groups),
        in_specs=[slab,
                  pl.BlockSpec(cs.shape, const2),
                  pl.BlockSpec(w1.shape, const2),
                  pl.BlockSpec(g2.shape, lambda i, g: (0, 0, 0))] + slices,
        out_specs=[slab] + slices,
        out_shape=[jax.ShapeDtypeStruct(f.shape, BF16)]
        + [jax.ShapeDtypeStruct(wt.shape, BF16) for wt in weights],
        scratch_shapes=[pltpu.VMEM((GRID_W * _pitch(rows), gd), F32),
                        pltpu.VMEM((GRID_W, 2 * rows, gd), BF16),
                        pltpu.VMEM((rows * GRID_W, gd), F32),
                        pltpu.VMEM((rows * GRID_W, gd), F32)],
        compiler_params=_params("arbitrary", "arbitrary"),
        name="fourier",
    )(f, cs, w1, g2, *weights)
    return out[0], out[1:]


def _fold_lanes(op, *arrays):
    blocks = [a[:, j:j + LANES] for a in arrays for j in range(0, a.shape[1], LANES)]
    while len(blocks) > 1:
        blocks = [op(blocks[i], blocks[i + 1]) if i + 1 < len(blocks) else blocks[i]
                  for i in range(0, len(blocks), 2)]
    return blocks[0]


def _build_bias(rpb_ref, bias_ref):
    n_dr = 2 * WIN_ROWS - 1
    col = lax.broadcasted_iota(jnp.int32, (GRID_W, LANES), 0)
    lane = lax.broadcasted_iota(jnp.int32, (GRID_W, LANES), 1)
    kcol = lane % GRID_W
    col_start = jnp.clip(col - WIN_COLS // 2, 0, GRID_W - WIN_COLS)
    valid = (kcol >= col_start) & (kcol < col_start + WIN_COLS)
    low = lane < GRID_W

    def per_head(h, carry):
        pair = h // 2
        row0 = pl.multiple_of((h % 2) * GRID_W, GRID_W)
        at_low, at_high = [], []
        for dr in range(n_dr):
            p = jnp.broadcast_to(rpb_ref[pl.ds(h * n_dr + dr, 1), :], (GRID_W, LANES))
            at_low.append(pltpu.roll(p, LANES - GRID_W + 1, 1, stride=1, stride_axis=0))
            at_high.append(pltpu.roll(p, LANES - 2 * GRID_W + 1, 1, stride=1, stride_axis=0))
        for variant in range(WIN_ROWS):
            for jp in range(WIN_ROWS // 2):
                dr = 2 * jp - variant + WIN_ROWS - 1
                block = jnp.where(valid, jnp.where(low, at_low[dr], at_high[dr + 1]), MASK_VALUE)
                bias_ref[variant, pair, pl.ds(row0, GRID_W), jp * LANES:(jp + 1) * LANES] = block
        return carry

    lax.fori_loop(0, NA_HEADS, per_head, 0)


def _attn_kernel(q_ref, k_ref, v_ref, kc_ref, vc_ref, rpb_ref, o_ref, bias_ref, *,
                 rows_per_step):
    @pl.when((pl.program_id(0) == 0) & (pl.program_id(1) == 0))
    def _():
        _build_bias(rpb_ref, bias_ref)

    n_rows = k_ref.shape[1] // GRID_W
    n_loc = WIN_ROWS * GRID_W
    lane = lax.broadcasted_iota(jnp.int32, (GRID_W, LANES), 1)
    low = lane < HEAD_DIM
    units = [(rr, pair) for rr in range(rows_per_step) for pair in range(NA_WIDTH // LANES)]

    def window(rr):
        r = pl.program_id(1) * rows_per_step + rr
        row_start = jnp.clip(r - WIN_ROWS // 2, 0, n_rows - WIN_ROWS)
        return r - row_start, pl.multiple_of(row_start * GRID_W, GRID_W)

    def scores(rr, pair):
        variant, start = window(rr)
        sl = slice(pair * LANES, (pair + 1) * LANES)
        qp = q_ref[0, rr * GRID_W:(rr + 1) * GRID_W, sl]
        zero = jnp.zeros_like(qp)
        q2 = jnp.concatenate([jnp.where(low, qp, zero), jnp.where(low, zero, qp)], axis=0)
        s_loc = _dot_nt(q2, k_ref[0, pl.ds(start, n_loc), sl]) + bias_ref[variant, pair]
        s_ctx = _dot_nt(q2, kc_ref[0, :, sl])
        return s_loc, s_ctx

    def finish(rr, pair, s_loc, s_ctx):
        _, start = window(rr)
        sl = slice(pair * LANES, (pair + 1) * LANES)
        m = jnp.max(_fold_lanes(jnp.maximum, s_loc, s_ctx), axis=-1, keepdims=True)
        p_loc = jnp.exp2(s_loc - m)
        p_ctx = jnp.exp2(s_ctx - m)
        denom = jnp.sum(_fold_lanes(jnp.add, p_loc, p_ctx), axis=-1, keepdims=True)
        o = (_dot(p_loc.astype(BF16), v_ref[0, pl.ds(start, n_loc), sl])
             + _dot(p_ctx.astype(BF16), vc_ref[0, :, sl]))
        o = o / denom
        o_ref[0, rr * GRID_W:(rr + 1) * GRID_W, sl] = (
            jnp.where(low, o[:GRID_W], o[GRID_W:]).astype(BF16))

    pending = [scores(*u) for u in units[:SCORE_LOOKAHEAD]]
    for i, unit in enumerate(units):
        if i + SCORE_LOOKAHEAD < len(units):
            pending.append(scores(*units[i + SCORE_LOOKAHEAD]))
        finish(*unit, *pending.pop(0))


def _attn(q, k, v, kc, vc, rpb_rows, rows_per_step):
    b, s, width = q.shape
    n_rows = s // GRID_W
    c = kc.shape[1]
    tq = rows_per_step * GRID_W
    whole = lambda i, r: (i, 0, 0)
    heads_per_block = LANES // HEAD_DIM
    bias_shape = (WIN_ROWS, NA_HEADS // heads_per_block, heads_per_block * GRID_W,
                  WIN_ROWS * GRID_W)
    return pl.pallas_call(
        functools.partial(_attn_kernel, rows_per_step=rows_per_step),
        grid=(b, n_rows // rows_per_step),
        in_specs=[pl.BlockSpec((1, tq, width), lambda i, r: (i, r, 0)),
                  pl.BlockSpec((1, s, width), whole),
                  pl.BlockSpec((1, s, width), whole),
                  pl.BlockSpec((1, c, width), whole),
                  pl.BlockSpec((1, c, width), whole),
                  pl.BlockSpec(rpb_rows.shape, lambda i, r: (0, 0))],
        out_specs=pl.BlockSpec((1, tq, width), lambda i, r: (i, r, 0)),
        out_shape=jax.ShapeDtypeStruct((b, s, width), BF16),
        scratch_shapes=[pltpu.VMEM(bias_shape, F32)],
        compiler_params=_params("arbitrary", "arbitrary"),
        name="attn",
    )(q, k, v, kc, vc, rpb_rows)


def _merge_kernel(x_ref, ya_ref, yf_ref, mod_ref, g1_ref, wg_ref, wfo_ref, wao_ref, wo_ref,
                  o_ref, *, chunk):
    d = g1_ref.shape[1]
    m = mod_ref[0]

    def branches(c):
        rows = slice(c * chunk, (c + 1) * chunk)
        h = _rms_mod(x_ref[0, rows, :], g1_ref[...], m[0:1], m[1:2]).astype(BF16)
        return (_dot(h, wg_ref[...]), _dot(yf_ref[0, rows, :], wfo_ref[...]),
                _dot(ya_ref[0, rows, :], wao_ref[...]))

    def finish(c, gate_logits, four, attn):
        rows = slice(c * chunk, (c + 1) * chunk)
        gates = jax.nn.sigmoid(gate_logits)
        merged = gates[:, :d] * four + gates[:, d:] * attn
        o_ref[0, rows, :] = x_ref[0, rows, :] + m[2:3] * _dot(merged.astype(BF16), wo_ref[...])

    n_chunks = x_ref.shape[1] // chunk
    pending = branches(0)
    for c in range(n_chunks):
        upcoming = branches(c + 1) if c + 1 < n_chunks else None
        finish(c, *pending)
        pending = upcoming


def _merge(x, y_attn, y_four, mod, g1, w_gate, w_fo, w_ao, w_o, tm):
    b, s, d = x.shape
    const = lambda i, t: (0, 0)
    tok = lambda i, t: (i, t, 0)
    return pl.pallas_call(
        functools.partial(_merge_kernel, chunk=256),
        grid=(b, s // tm),
        in_specs=[pl.BlockSpec((1, tm, d), tok),
                  pl.BlockSpec((1, tm, NA_WIDTH), tok),
                  pl.BlockSpec((1, tm, FOURIER_WIDTH), tok),
                  pl.BlockSpec((1, 8, d), lambda i, t: (i, 0, 0)),
                  pl.BlockSpec((1, d), const),
                  pl.BlockSpec(w_gate.shape, const),
                  pl.BlockSpec(w_fo.shape, const),
                  pl.BlockSpec(w_ao.shape, const),
                  pl.BlockSpec(w_o.shape, const)],
        out_specs=pl.BlockSpec((1, tm, d), tok),
        out_shape=jax.ShapeDtypeStruct(x.shape, F32),
        compiler_params=_params("arbitrary", "arbitrary"),
        name="merge",
    )(x, y_attn, y_four, mod, g1, w_gate, w_fo, w_ao, w_o)


def _mlp_kernel(x_ref, mod_ref, g2_ref, w1_ref, w2_ref, o_ref, *, ff_chunk):
    x = x_ref[0]
    m = mod_ref[0]
    h = _rms_mod(x, g2_ref[...], m[3:4], m[4:5]).astype(BF16)
    acc = jnp.zeros(x.shape, F32)
    for c in range(w1_ref.shape[1] // ff_chunk):
        a = jnp.maximum(_dot(h, w1_ref[:, c * ff_chunk:(c + 1) * ff_chunk]), 0.0)
        acc = acc + _dot((a * a).astype(BF16), w2_ref[c * ff_chunk:(c + 1) * ff_chunk, :])
    o_ref[0] = x + m[5:6] * acc


def _mlp(x, mod, g2, w1, w2, tm, ff_chunk):
    b, s, d = x.shape
    const = lambda i, t: (0, 0)
    return pl.pallas_call(
        functools.partial(_mlp_kernel, ff_chunk=ff_chunk),
        grid=(b, s // tm),
        in_specs=[pl.BlockSpec((1, tm, d), lambda i, t: (i, t, 0)),
                  pl.BlockSpec((1, 8, d), lambda i, t: (i, 0, 0)),
                  pl.BlockSpec((1, d), const),
                  pl.BlockSpec(w1.shape, const, pipeline_mode=pl.Buffered(1)),
                  pl.BlockSpec(w2.shape, const, pipeline_mode=pl.Buffered(1))],
        out_specs=pl.BlockSpec((1, tm, d), lambda i, t: (i, t, 0)),
        out_shape=jax.ShapeDtypeStruct(x.shape, F32),
        compiler_params=_params("arbitrary", "arbitrary"),
        name="mlp",
    )(x, mod, g2, w1, w2)


def _dft_tables(rows):
    n = rows * GRID_W
    gd = FOURIER_GROUP_DIM
    c = np.arange(gd)
    ang = 2.0 * np.pi * np.outer(c, c) / gd
    cs = np.concatenate([np.cos(ang), np.sin(ang)], axis=1) * 2.0 ** -3
    cs = np.kron(np.eye(2), cs)
    a = np.arange(rows)
    ang = 2.0 * np.pi * np.outer(a, a) / rows
    c1, s1 = np.cos(ang), np.sin(ang)
    w1 = np.block([[c1, -s1], [s1, c1]]) * 2.0 ** -4
    b = np.arange(GRID_W)
    k = a[:, None] + rows * b[None, :]
    ang = 2.0 * np.pi * k[:, :, None] * b[None, None, :] / n
    g2 = np.concatenate([np.cos(ang), -np.sin(ang)], axis=2) * 2.0 ** -3
    assert gd * rows * GRID_W == 2 ** 20
    return tuple(jnp.asarray(m, F32).astype(BF16) for m in (cs, w1, g2))


def _rope_tables(seq):
    t = np.arange(seq)
    axis_dim = HEAD_DIM // 2
    inv_freq = ROPE_THETA ** (-np.arange(0, axis_dim, 2, dtype=np.float64) / axis_dim)
    ang_r = (t // GRID_W)[:, None] * inv_freq
    ang_c = (t % GRID_W)[:, None] * inv_freq
    ang = np.concatenate([ang_r, ang_r, ang_c, ang_c], axis=1)
    sign = np.tile(np.repeat([-1.0, 1.0], axis_dim // 2), 2)
    reps = LANES // HEAD_DIM
    cos = np.tile(np.cos(ang), (1, reps))
    sin = np.tile(np.sin(ang) * sign, (1, reps))
    return jnp.asarray(cos, F32), jnp.asarray(sin, F32)


def _rpb_rows(rpb):
    left = GRID_W - WIN_COLS
    padded = jnp.pad(rpb * LOG2_E, ((0, 0), (0, 0), (left, LANES - left - rpb.shape[-1])))
    return padded.reshape(rpb.shape[0] * rpb.shape[1], LANES)


def kernel(x, c, ctx, c_ctx, w_ada, b_ada, norm1_g, norm2_g, w_in, q_norm_g, k_norm_g, rpb,
           w_branch_gate, w_fourier_out, w_attn_out, w_out, w_mlp1, w_mlp2):
    b, s, d = x.shape
    assert w_ada.shape[0] == 1, "single-layer block only"
    n_rows = s // GRID_W
    assert n_rows >= WIN_ROWS and b < 8

    cond = jnp.zeros((8, d), F32).at[:b].set(c).at[b].set(c_ctx)
    mods = _ada(cond, w_ada[0], b_ada)
    mod = jnp.pad(mods[:b].reshape(b, N_MOD, d), ((0, 0), (0, 8 - N_MOD), (0, 0)))
    mod_ctx = jnp.pad(mods[b].reshape(N_MOD, d), ((0, 8 - N_MOD), (0, 0)))

    half = NA_WIDTH // 2
    head_mean = jnp.asarray(
        np.kron(np.eye(half // HEAD_DIM), np.full((HEAD_DIM, HEAD_DIM), 1.0 / HEAD_DIM)),
        F32).astype(BF16)
    qg = jnp.tile(q_norm_g[0], NA_HEADS)[None, :] * (HEAD_DIM ** -0.5 * LOG2_E)
    kg = jnp.tile(k_norm_g[0], NA_HEADS)[None, :]
    cos, sin = _rope_tables(s)
    cs, w1, g2 = _dft_tables(n_rows)
    w_in_b = w_in[0].astype(BF16)

    kc, vc = _ctx_kv(ctx, mod_ctx, norm1_g, w_in_b, kg, head_mean)
    f, q, k, v = _proj(x, mod, norm1_g, w_in_b, qg, kg, head_mean, cos, sin, tm=1024)
    y_four, (w_gate_b, w_fo_b, w_ao_b, w_o_b, w_mlp1_b, w_mlp2_b) = _fourier(
        f, cs, w1, g2, [w_branch_gate[0], w_fourier_out[0], w_attn_out[0], w_out[0],
                        w_mlp1[0], w_mlp2[0]])
    y_attn = _attn(q, k, v, kc, vc, _rpb_rows(rpb[0]), rows_per_step=8)
    x1 = _merge(x, y_attn, y_four, mod, norm1_g, w_gate_b, w_fo_b, w_ao_b, w_o_b, tm=1024)
    return _mlp(x1, mod, norm2_g, w_mlp1_b, w_mlp2_b, tm=1024, ff_chunk=1024)
```

```python
import functools

import numpy as np
import jax
import jax.numpy as jnp
from jax import lax
from jax.experimental import pallas as pl
from jax.experimental.pallas import tpu as pltpu

GRID_W = 64
N_FOURIER_GROUPS = 4
FOURIER_GROUP_DIM = 128
FOURIER_WIDTH = N_FOURIER_GROUPS * FOURIER_GROUP_DIM
NA_HEADS = 8
HEAD_DIM = 64
NA_WIDTH = NA_HEADS * HEAD_DIM
WIN_ROWS = 8
WIN_COLS = 16
ROPE_THETA = 10000.0
NORM_EPS = 1e-6
N_MOD = 6
MASK_VALUE = -1e30
LOG2_E = 1.4426950408889634
SCORE_LOOKAHEAD = 2
ROW_COLS = 4

N_FREQ = HEAD_DIM // 4

LANES = 128
SUBLANES = 8
VMEM_LIMIT_BYTES = 48 * 1024 * 1024

HALF_LANES = LANES // 2
HEAD_LANES = HALF_LANES // (LANES // HEAD_DIM)

BF16 = jnp.bfloat16
F32 = jnp.float32


def _dot(a, b):
    return jnp.dot(a, b, preferred_element_type=F32)


def _dot_nt(a, b):
    return lax.dot_general(a, b, (((1,), (1,)), ((), ())), preferred_element_type=F32)


def _params(*semantics):
    return pltpu.CompilerParams(dimension_semantics=semantics,
                                vmem_limit_bytes=VMEM_LIMIT_BYTES)


def _rms_mod(x, g, shift, scale):
    ms = jnp.mean(x * x, axis=-1, keepdims=True)
    return (x * lax.rsqrt(ms + NORM_EPS) * g) * (1.0 + scale) + shift


def _head_norm(a, gain, head_mean):
    half = NA_WIDTH // 2
    sq = (a * a).astype(BF16)
    ms = jnp.concatenate([_dot(sq[:, :half], head_mean), _dot(sq[:, half:], head_mean)], axis=1)
    return a * lax.rsqrt(ms + NORM_EPS) * gain


def _ada_kernel(cond_ref, w_ref, b_ref, o_ref):
    cond = cond_ref[...]
    s = cond * jax.nn.sigmoid(cond)
    s_hi = s.astype(BF16)
    s_lo = (s - s_hi.astype(F32)).astype(BF16)
    w = w_ref[...]
    w_hi = w.astype(BF16)
    w_lo = (w - w_hi.astype(F32)).astype(BF16)
    o_ref[...] = _dot(s_hi, w_hi) + _dot(s_hi, w_lo) + _dot(s_lo, w_hi) + b_ref[...]


def _ada(cond, w, b):
    rows, d = cond.shape
    n = w.shape[1]
    tn = 1536
    return pl.pallas_call(
        _ada_kernel,
        grid=(n // tn,),
        in_specs=[pl.BlockSpec((rows, d), lambda j: (0, 0)),
                  pl.BlockSpec((d, tn), lambda j: (0, j)),
                  pl.BlockSpec((1, tn), lambda j: (0, j))],
        out_specs=pl.BlockSpec((rows, tn), lambda j: (0, j)),
        out_shape=jax.ShapeDtypeStruct((rows, n), F32),
        compiler_params=_params("arbitrary"),
        name="ada",
    )(cond, w, b)


def _w_in_kernel(w_ref, p_ref, o_ref):
    wb = w_ref[...].astype(BF16)
    v_col = FOURIER_WIDTH + 2 * NA_WIDTH
    o_ref[:, :FOURIER_WIDTH] = wb[:, :FOURIER_WIDTH]
    o_ref[:, v_col:] = wb[:, v_col:]
    for col in (FOURIER_WIDTH, FOURIER_WIDTH + NA_WIDTH):
        o_ref[:, col:col + NA_WIDTH] = _dot(wb[:, col:col + NA_WIDTH], p_ref[...]).astype(BF16)


def _w_in_bf16(w_in, perm_matrix, tr):
    d, n = w_in.shape
    return pl.pallas_call(
        _w_in_kernel,
        grid=(d // tr,),
        in_specs=[pl.BlockSpec((tr, n), lambda i: (i, 0)),
                  pl.BlockSpec(perm_matrix.shape, lambda i: (0, 0))],
        out_specs=pl.BlockSpec((tr, n), lambda i: (i, 0)),
        out_shape=jax.ShapeDtypeStruct((d, n), BF16),
        compiler_params=_params("arbitrary"),
        name="w_in",
    )(w_in, perm_matrix)


def _ctx_kv_kernel(ctx_ref, mod_ref, g1_ref, w_ref, kg_ref, hm_ref, k_ref, v_ref):
    m = mod_ref[...]
    h = _rms_mod(ctx_ref[0], g1_ref[...], m[0:1], m[1:2])
    p = _dot(h.astype(BF16), w_ref[...])
    k_ref[0] = _head_norm(p[:, :NA_WIDTH], kg_ref[...], hm_ref[...]).astype(BF16)
    v_ref[0] = p[:, NA_WIDTH:].astype(BF16)


def _ctx_kv(ctx, mod_ctx, g1, w_in, kg, head_mean):
    b, c, d = ctx.shape
    kv_cols = 2 * NA_WIDTH
    out = jax.ShapeDtypeStruct((b, c, NA_WIDTH), BF16)
    return pl.pallas_call(
        _ctx_kv_kernel,
        grid=(b,),
        in_specs=[pl.BlockSpec((1, c, d), lambda i: (i, 0, 0)),
                  pl.BlockSpec((8, d), lambda i: (0, 0)),
                  pl.BlockSpec((1, d), lambda i: (0, 0)),
                  pl.BlockSpec((d, kv_cols), lambda i: (0, (w_in.shape[1] - kv_cols) // kv_cols)),
                  pl.BlockSpec((1, NA_WIDTH), lambda i: (0, 0)),
                  pl.BlockSpec(head_mean.shape, lambda i: (0, 0))],
        out_specs=[pl.BlockSpec((1, c, NA_WIDTH), lambda i: (i, 0, 0))] * 2,
        out_shape=[out, out],
        compiler_params=_params("arbitrary"),
        name="ctx_kv",
    )(ctx, mod_ctx, g1, w_in, kg, head_mean)


def _rope(a, cos, sin_signed):
    return a * cos + pltpu.roll(a, HALF_LANES, 1) * sin_signed


def _proj_kernel(x_ref, mod_ref, g1_ref, w_ref, qg_ref, kg_ref, hm_ref, cos_ref, sin_ref,
                 f_ref, q_ref, k_ref, v_ref, *, chunk):
    m = mod_ref[0]
    hm = hm_ref[...]
    col_f, col_q, col_k, col_v = (0, FOURIER_WIDTH, FOURIER_WIDTH + NA_WIDTH,
                                  FOURIER_WIDTH + 2 * NA_WIDTH)

    def project(c):
        rows = slice(c * chunk, (c + 1) * chunk)
        h = _rms_mod(x_ref[0, rows, :], g1_ref[...], m[0:1], m[1:2]).astype(BF16)
        return tuple(_dot(h, w_ref[:, col:col + NA_WIDTH]) for col in (col_q, col_k, col_f, col_v))

    def finish(c, pq, pk, pf, pv):
        rows = slice(c * chunk, (c + 1) * chunk)
        cos = cos_ref[rows, :]
        sin = sin_ref[rows, :]
        for p, gain_ref, dst in ((pq, qg_ref, q_ref), (pk, kg_ref, k_ref)):
            a = _head_norm(p, gain_ref[...], hm)
            blocks = [_rope(a[:, j:j + LANES], cos, sin) for j in range(0, NA_WIDTH, LANES)]
            dst[0, rows, :] = jnp.concatenate(blocks, axis=1).astype(BF16)
        f_ref[0, rows, :] = pf.astype(BF16)
        v_ref[0, rows, :] = pv.astype(BF16)

    n_chunks = x_ref.shape[1] // chunk
    pending = project(0)
    for c in range(n_chunks):
        upcoming = project(c + 1) if c + 1 < n_chunks else None
        finish(c, *pending)
        pending = upcoming


def _proj(x, mod, g1, w_in, qg, kg, head_mean, cos, sin, tm):
    b, s, d = x.shape
    out = jax.ShapeDtypeStruct((b, s, NA_WIDTH), BF16)
    tok = lambda t, i: (i, t, 0)
    const = lambda t, i: (0, 0)
    return pl.pallas_call(
        functools.partial(_proj_kernel, chunk=128),
        grid=(s // tm, b),
        in_specs=[pl.BlockSpec((1, tm, d), tok),
                  pl.BlockSpec((1, 8, d), lambda t, i: (i, 0, 0)),
                  pl.BlockSpec((1, d), const),
                  pl.BlockSpec(w_in.shape, const),
                  pl.BlockSpec((1, NA_WIDTH), const),
                  pl.BlockSpec((1, NA_WIDTH), const),
                  pl.BlockSpec(head_mean.shape, const),
                  pl.BlockSpec((tm, cos.shape[1]), lambda t, i: (t, 0)),
                  pl.BlockSpec((tm, sin.shape[1]), lambda t, i: (t, 0))],
        out_specs=[pl.BlockSpec((1, tm, NA_WIDTH), tok)] * 4,
        out_shape=[out] * 4,
        compiler_params=_params("arbitrary", "arbitrary"),
        name="proj",
    )(x, mod, g1, w_in, qg, kg, head_mean, cos, sin)


def _pitch(n):
    tiles = -(-n // SUBLANES)
    return SUBLANES * (tiles + 1 - tiles % 2)


def _fourier_kernel(*refs, rows, b_chunk, n_cast):
    u_ref, cs_ref, w1_ref, g_ref = refs[:4]
    cast_src = refs[4:4 + n_cast]
    y_ref = refs[4 + n_cast]
    cast_dst = refs[5 + n_cast:5 + 2 * n_cast]
    ut_ref, z_ref, tr_ref, ti_ref = refs[5 + 2 * n_cast:]
    gd = FOURIER_GROUP_DIM
    w = GRID_W
    row_pitch = _pitch(rows)

    def reorder(a, carry):
        base = pl.multiple_of(a * w, w)
        ut_ref[pl.ds(a, w, stride=row_pitch), :] = u_ref[0, pl.ds(base, w), :].astype(F32)
        return carry

    def column(b):
        return ut_ref[pl.ds(pl.multiple_of(b * row_pitch, SUBLANES), rows), :].astype(BF16)

    def channel_stage(i, carry):
        b0 = i * b_chunk
        lhs = jnp.concatenate(
            [jnp.concatenate([column(b0 + 2 * p), column(b0 + 2 * p + 1)], axis=1)
             for p in range(b_chunk // 2)], axis=0)
        zc = _dot(lhs, cs_ref[...])
        for t in range(b_chunk):
            block = zc[(t // 2) * rows:(t // 2 + 1) * rows, (t % 2) * 2 * gd:(t % 2 + 1) * 2 * gd]
            z_ref[b0 + t, :rows] = block[:, :gd].astype(BF16)
            z_ref[b0 + t, rows:] = block[:, gd:].astype(BF16)
        return carry

    def row_stage(i, carry):
        t2 = _dot(w1_ref[...],
                  jnp.concatenate([z_ref[ROW_COLS * i + t] for t in range(ROW_COLS)], axis=1))
        for t in range(ROW_COLS):
            b = ROW_COLS * i + t
            for blk in range(rows // SUBLANES):
                dst = pl.multiple_of((blk * w + b) * SUBLANES, SUBLANES)
                lo = blk * SUBLANES
                tr_ref[pl.ds(dst, SUBLANES), :] = t2[lo:lo + SUBLANES, t * gd:(t + 1) * gd]
                ti_ref[pl.ds(dst, SUBLANES), :] = t2[rows + lo:rows + lo + SUBLANES,
                                                     t * gd:(t + 1) * gd]
        return carry

    def col_stage(blk, carry):
        for r_in in range(SUBLANES):
            ka = blk * SUBLANES + r_in
            start = blk * (w * SUBLANES) + r_in
            rhs = jnp.concatenate([tr_ref[pl.ds(start, w, stride=SUBLANES), :],
                                   ti_ref[pl.ds(start, w, stride=SUBLANES), :]], axis=0)
            y = _dot(g_ref[ka], rhs.astype(BF16))
            for kblk in range(w // SUBLANES):
                dst = pl.multiple_of((kblk * rows + ka) * SUBLANES, SUBLANES)
                ut_ref[pl.ds(dst, SUBLANES), :] = y[kblk * SUBLANES:(kblk + 1) * SUBLANES]
        return carry

    def emit(kblk, carry):
        for r_in in range(SUBLANES):
            kb = kblk * SUBLANES + r_in
            start = kblk * (rows * SUBLANES) + r_in
            dst = pl.multiple_of(kb * rows, rows)
            y_ref[0, pl.ds(dst, rows), :] = (
                ut_ref[pl.ds(start, rows, stride=SUBLANES), :].astype(BF16))
        return carry

    lax.fori_loop(0, rows, reorder, 0, unroll=8)
    for src, dst in zip(cast_src, cast_dst):
        dst[...] = src[...].astype(BF16)
    for i in range(w // b_chunk):
        channel_stage(i, 0)
    for i in range(w // ROW_COLS):
        row_stage(i, 0)
    lax.fori_loop(0, rows // SUBLANES, col_stage, 0, unroll=2)
    lax.fori_loop(0, w // SUBLANES, emit, 0)


def _fourier(f, cs, w1, g2, weights):
    b, s, width = f.shape
    rows = s // GRID_W
    gd = FOURIER_GROUP_DIM
    n_groups = width // gd
    n_steps = b * n_groups
    slab = pl.BlockSpec((1, s, gd), lambda i, g: (i, 0, g))
    const2 = lambda i, g: (0, 0)
    slices = [pl.BlockSpec((wt.shape[0] // n_steps, wt.shape[1]),
                           lambda i, g: (i * n_groups + g, 0)) for wt in weights]
    assert all(wt.shape[0] % (n_steps * 2 * SUBLANES) == 0 for wt in weights)
    out = pl.pallas_call(
        functools.partial(_fourier_kernel, rows=rows, b_chunk=8, n_cast=len(weights)),
        grid=(b, n_groups),
        in_specs=[slab,
                  pl.BlockSpec(cs.shape, const2),
                  pl.BlockSpec(w1.shape, const2),
                  pl.BlockSpec(g2.shape, lambda i, g: (0, 0, 0))] + slices,
        out_specs=[slab] + slices,
        out_shape=[jax.ShapeDtypeStruct(f.shape, BF16)]
        + [jax.ShapeDtypeStruct(wt.shape, BF16) for wt in weights],
        scratch_shapes=[pltpu.VMEM((GRID_W * _pitch(rows), gd), F32),
                        pltpu.VMEM((GRID_W, 2 * rows, gd), BF16),
                        pltpu.VMEM((rows * GRID_W, gd), F32),
                        pltpu.VMEM((rows * GRID_W, gd), F32)],
        compiler_params=_params("arbitrary", "arbitrary"),
        name="fourier",
    )(f, cs, w1, g2, *weights)
    return out[0], out[1:]


def _fold_lanes(op, *arrays):
    blocks = [a[:, j:j + LANES] for a in arrays for j in range(0, a.shape[1], LANES)]
    while len(blocks) > 1:
        blocks = [op(blocks[i], blocks[i + 1]) if i + 1 < len(blocks) else blocks[i]
                  for i in range(0, len(blocks), 2)]
    return blocks[0]


def _build_bias(rpb_ref, bias_ref):
    n_dr = 2 * WIN_ROWS - 1
    col = lax.broadcasted_iota(jnp.int32, (GRID_W, LANES), 0)
    lane = lax.broadcasted_iota(jnp.int32, (GRID_W, LANES), 1)
    kcol = lane % GRID_W
    col_start = jnp.clip(col - WIN_COLS // 2, 0, GRID_W - WIN_COLS)
    valid = (kcol >= col_start) & (kcol < col_start + WIN_COLS)
    low = lane < GRID_W

    def per_head(h, carry):
        pair = h // 2
        row0 = pl.multiple_of((h % 2) * GRID_W, GRID_W)
        at_low, at_high = [], []
        for dr in range(n_dr):
            p = jnp.broadcast_to(rpb_ref[pl.ds(h * n_dr + dr, 1), :], (GRID_W, LANES))
            at_low.append(pltpu.roll(p, LANES - GRID_W + 1, 1, stride=1, stride_axis=0))
            at_high.append(pltpu.roll(p, LANES - 2 * GRID_W + 1, 1, stride=1, stride_axis=0))
        for variant in range(WIN_ROWS):
            for jp in range(WIN_ROWS // 2):
                dr = 2 * jp - variant + WIN_ROWS - 1
                block = jnp.where(valid, jnp.where(low, at_low[dr], at_high[dr + 1]), MASK_VALUE)
                bias_ref[variant, pair, pl.ds(row0, GRID_W), jp * LANES:(jp + 1) * LANES] = block
        return carry

    lax.fori_loop(0, NA_HEADS, per_head, 0)


def _attn_kernel(q_ref, k_ref, v_ref, kc_ref, vc_ref, rpb_ref, o_ref, bias_ref, *,
                 rows_per_step):
    @pl.when((pl.program_id(0) == 0) & (pl.program_id(1) == 0))
    def _():
        _build_bias(rpb_ref, bias_ref)

    n_rows = k_ref.shape[1] // GRID_W
    n_loc = WIN_ROWS * GRID_W
    lane = lax.broadcasted_iota(jnp.int32, (GRID_W, LANES), 1)
    low = lane < HEAD_DIM
    first = (lane % HALF_LANES) < HEAD_LANES
    units = [(rr, pair) for rr in range(rows_per_step) for pair in range(NA_WIDTH // LANES)]

    def window(rr):
        r = pl.program_id(1) * rows_per_step + rr
        row_start = jnp.clip(r - WIN_ROWS // 2, 0, n_rows - WIN_ROWS)
        return r - row_start, pl.multiple_of(row_start * GRID_W, GRID_W)

    def scores(rr, pair):
        variant, start = window(rr)
        sl = slice(pair * LANES, (pair + 1) * LANES)
        qp = q_ref[0, rr * GRID_W:(rr + 1) * GRID_W, sl]
        zero = jnp.zeros_like(qp)
        q2 = jnp.concatenate([jnp.where(first, qp, zero), jnp.where(first, zero, qp)], axis=0)
        s_loc = _dot_nt(q2, k_ref[0, pl.ds(start, n_loc), sl]) + bias_ref[variant, pair]
        s_ctx = _dot_nt(q2, kc_ref[0, :, sl])
        return s_loc, s_ctx

    def finish(rr, pair, s_loc, s_ctx):
        _, start = window(rr)
        sl = slice(pair * LANES, (pair + 1) * LANES)
        m = jnp.max(_fold_lanes(jnp.maximum, s_loc, s_ctx), axis=-1, keepdims=True)
        p_loc = jnp.exp2(s_loc - m)
        p_ctx = jnp.exp2(s_ctx - m)
        denom = jnp.sum(_fold_lanes(jnp.add, p_loc, p_ctx), axis=-1, keepdims=True)
        o = (_dot(p_loc.astype(BF16), v_ref[0, pl.ds(start, n_loc), sl])
             + _dot(p_ctx.astype(BF16), vc_ref[0, :, sl]))
        o = o / denom
        o_ref[0, rr * GRID_W:(rr + 1) * GRID_W, sl] = (
            jnp.where(low, o[:GRID_W], o[GRID_W:]).astype(BF16))

    pending = [scores(*u) for u in units[:SCORE_LOOKAHEAD]]
    for i, unit in enumerate(units):
        if i + SCORE_LOOKAHEAD < len(units):
            pending.append(scores(*units[i + SCORE_LOOKAHEAD]))
        finish(*unit, *pending.pop(0))


def _attn(q, k, v, kc, vc, rpb_rows, rows_per_step):
    b, s, width = q.shape
    n_rows = s // GRID_W
    c = kc.shape[1]
    tq = rows_per_step * GRID_W
    whole = lambda i, r: (i, 0, 0)
    heads_per_block = LANES // HEAD_DIM
    bias_shape = (WIN_ROWS, NA_HEADS // heads_per_block, heads_per_block * GRID_W,
                  WIN_ROWS * GRID_W)
    return pl.pallas_call(
        functools.partial(_attn_kernel, rows_per_step=rows_per_step),
        grid=(b, n_rows // rows_per_step),
        in_specs=[pl.BlockSpec((1, tq, width), lambda i, r: (i, r, 0)),
                  pl.BlockSpec((1, s, width), whole),
                  pl.BlockSpec((1, s, width), whole),
                  pl.BlockSpec((1, c, width), whole),
                  pl.BlockSpec((1, c, width), whole),
                  pl.BlockSpec(rpb_rows.shape, lambda i, r: (0, 0))],
        out_specs=pl.BlockSpec((1, tq, width), lambda i, r: (i, r, 0)),
        out_shape=jax.ShapeDtypeStruct((b, s, width), BF16),
        scratch_shapes=[pltpu.VMEM(bias_shape, F32)],
        compiler_params=_params("arbitrary", "arbitrary"),
        name="attn",
    )(q, k, v, kc, vc, rpb_rows)


def _merge_kernel(x_ref, ya_ref, yf_ref, mod_ref, g1_ref, wg_ref, wfo_ref, wao_ref, wo_ref,
                  o_ref, *, chunk):
    d = g1_ref.shape[1]
    m = mod_ref[0]

    def branches(c):
        rows = slice(c * chunk, (c + 1) * chunk)
        h = _rms_mod(x_ref[0, rows, :], g1_ref[...], m[0:1], m[1:2]).astype(BF16)
        return (_dot(h, wg_ref[...]), _dot(yf_ref[0, rows, :], wfo_ref[...]),
                _dot(ya_ref[0, rows, :], wao_ref[...]))

    def finish(c, gate_logits, four, attn):
        rows = slice(c * chunk, (c + 1) * chunk)
        gates = jax.nn.sigmoid(gate_logits)
        merged = gates[:, :d] * four + gates[:, d:] * attn
        o_ref[0, rows, :] = x_ref[0, rows, :] + m[2:3] * _dot(merged.astype(BF16), wo_ref[...])

    n_chunks = x_ref.shape[1] // chunk
    pending = branches(0)
    for c in range(n_chunks):
        upcoming = branches(c + 1) if c + 1 < n_chunks else None
        finish(c, *pending)
        pending = upcoming


def _merge(x, y_attn, y_four, mod, g1, w_gate, w_fo, w_ao, w_o, tm):
    b, s, d = x.shape
    const = lambda i, t: (0, 0)
    tok = lambda i, t: (i, t, 0)
    return pl.pallas_call(
        functools.partial(_merge_kernel, chunk=256),
        grid=(b, s // tm),
        in_specs=[pl.BlockSpec((1, tm, d), tok),
                  pl.BlockSpec((1, tm, NA_WIDTH), tok),
                  pl.BlockSpec((1, tm, FOURIER_WIDTH), tok),
                  pl.BlockSpec((1, 8, d), lambda i, t: (i, 0, 0)),
                  pl.BlockSpec((1, d), const),
                  pl.BlockSpec(w_gate.shape, const),
                  pl.BlockSpec(w_fo.shape, const),
                  pl.BlockSpec(w_ao.shape, const),
                  pl.BlockSpec(w_o.shape, const)],
        out_specs=pl.BlockSpec((1, tm, d), tok),
        out_shape=jax.ShapeDtypeStruct(x.shape, F32),
        compiler_params=_params("arbitrary", "arbitrary"),
        name="merge",
    )(x, y_attn, y_four, mod, g1, w_gate, w_fo, w_ao, w_o)


def _mlp_kernel(x_ref, mod_ref, g2_ref, w1_ref, w2_ref, o_ref, *, ff_chunk):
    x = x_ref[0]
    m = mod_ref[0]
    h = _rms_mod(x, g2_ref[...], m[3:4], m[4:5]).astype(BF16)
    acc = jnp.zeros(x.shape, F32)
    for c in range(w1_ref.shape[1] // ff_chunk):
        a = jnp.maximum(_dot(h, w1_ref[:, c * ff_chunk:(c + 1) * ff_chunk]), 0.0)
        acc = acc + _dot((a * a).astype(BF16), w2_ref[c * ff_chunk:(c + 1) * ff_chunk, :])
    o_ref[0] = x + m[5:6] * acc


def _mlp(x, mod, g2, w1, w2, tm, ff_chunk):
    b, s, d = x.shape
    const = lambda i, t: (0, 0)
    return pl.pallas_call(
        functools.partial(_mlp_kernel, ff_chunk=ff_chunk),
        grid=(b, s // tm),
        in_specs=[pl.BlockSpec((1, tm, d), lambda i, t: (i, t, 0)),
                  pl.BlockSpec((1, 8, d), lambda i, t: (i, 0, 0)),
                  pl.BlockSpec((1, d), const),
                  pl.BlockSpec(w1.shape, const, pipeline_mode=pl.Buffered(1)),
                  pl.BlockSpec(w2.shape, const, pipeline_mode=pl.Buffered(1))],
        out_specs=pl.BlockSpec((1, tm, d), lambda i, t: (i, t, 0)),
        out_shape=jax.ShapeDtypeStruct(x.shape, F32),
        compiler_params=_params("arbitrary", "arbitrary"),
        name="mlp",
    )(x, mod, g2, w1, w2)


def _dft_tables(rows):
    n = rows * GRID_W
    gd = FOURIER_GROUP_DIM
    c = np.arange(gd)
    ang = 2.0 * np.pi * np.outer(c, c) / gd
    cs = np.concatenate([np.cos(ang), np.sin(ang)], axis=1) * 2.0 ** -3
    cs = np.kron(np.eye(2), cs)
    a = np.arange(rows)
    ang = 2.0 * np.pi * np.outer(a, a) / rows
    c1, s1 = np.cos(ang), np.sin(ang)
    w1 = np.block([[c1, -s1], [s1, c1]]) * 2.0 ** -4
    b = np.arange(GRID_W)
    k = a[:, None] + rows * b[None, :]
    ang = 2.0 * np.pi * k[:, :, None] * b[None, None, :] / n
    g2 = np.concatenate([np.cos(ang), -np.sin(ang)], axis=2) * 2.0 ** -3
    assert gd * rows * GRID_W == 2 ** 20
    return tuple(jnp.asarray(m, F32).astype(BF16) for m in (cs, w1, g2))


def _qk_lane_perm():
    lane = np.arange(NA_WIDTH)
    pair, in_block = lane // LANES, lane % LANES
    side, in_side = in_block // HALF_LANES, in_block % HALF_LANES
    head = pair * (LANES // HEAD_DIM) + in_side // HEAD_LANES
    axis, freq = (in_side % HEAD_LANES) // N_FREQ, in_side % N_FREQ
    return head * HEAD_DIM + axis * (2 * N_FREQ) + side * N_FREQ + freq


def _rope_tables(seq):
    t = np.arange(seq)
    axis_dim = 2 * N_FREQ
    inv_freq = ROPE_THETA ** (-np.arange(0, axis_dim, 2, dtype=np.float64) / axis_dim)
    ang_r = (t // GRID_W)[:, None] * inv_freq
    ang_c = (t % GRID_W)[:, None] * inv_freq
    ang = np.tile(np.concatenate([ang_r, ang_c], axis=1), (1, LANES // HEAD_LANES))
    sign = np.repeat([-1.0, 1.0], HALF_LANES)
    return jnp.asarray(np.cos(ang), F32), jnp.asarray(np.sin(ang) * sign, F32)


def _rpb_rows(rpb):
    left = GRID_W - WIN_COLS
    padded = jnp.pad(rpb * LOG2_E, ((0, 0), (0, 0), (left, LANES - left - rpb.shape[-1])))
    return padded.reshape(rpb.shape[0] * rpb.shape[1], LANES)


def kernel(x, c, ctx, c_ctx, w_ada, b_ada, norm1_g, norm2_g, w_in, q_norm_g, k_norm_g, rpb,
           w_branch_gate, w_fourier_out, w_attn_out, w_out, w_mlp1, w_mlp2):
    b, s, d = x.shape
    assert w_ada.shape[0] == 1, "single-layer block only"
    n_rows = s // GRID_W
    assert n_rows >= WIN_ROWS and b < 8

    cond = jnp.zeros((8, d), F32).at[:b].set(c).at[b].set(c_ctx)
    mods = _ada(cond, w_ada[0], b_ada)
    mod = jnp.pad(mods[:b].reshape(b, N_MOD, d), ((0, 0), (0, 8 - N_MOD), (0, 0)))
    mod_ctx = jnp.pad(mods[b].reshape(N_MOD, d), ((0, 8 - N_MOD), (0, 0)))

    perm = _qk_lane_perm()
    head_id = perm[:NA_WIDTH // 2] // HEAD_DIM
    head_mean = jnp.asarray((head_id[:, None] == head_id[None, :]) / HEAD_DIM, F32).astype(BF16)
    perm_matrix = jnp.asarray(np.arange(NA_WIDTH)[:, None] == perm[None, :], F32).astype(BF16)
    qg = q_norm_g[0][perm % HEAD_DIM][None, :] * (HEAD_DIM ** -0.5 * LOG2_E)
    kg = k_norm_g[0][perm % HEAD_DIM][None, :]
    cos, sin = _rope_tables(s)
    cs, w1, g2 = _dft_tables(n_rows)
    w_in_b = _w_in_bf16(w_in[0], perm_matrix, tr=256)

    kc, vc = _ctx_kv(ctx, mod_ctx, norm1_g, w_in_b, kg, head_mean)
    f, q, k, v = _proj(x, mod, norm1_g, w_in_b, qg, kg, head_mean, cos, sin, tm=1024)
    y_four, (w_gate_b, w_fo_b, w_ao_b, w_o_b, w_mlp1_b, w_mlp2_b) = _fourier(
        f, cs, w1, g2, [w_branch_gate[0], w_fourier_out[0], w_attn_out[0], w_out[0],
                        w_mlp1[0], w_mlp2[0]])
    y_attn = _attn(q, k, v, kc, vc, _rpb_rows(rpb[0]), rows_per_step=8)
    x1 = _merge(x, y_attn, y_four, mod, norm1_g, w_gate_b, w_fo_b, w_ao_b, w_o_b, tm=1024)
    return _mlp(x1, mod, norm2_g, w_mlp1_b, w_mlp2_b, tm=1024, ff_chunk=1024)
```

```python
import functools

import numpy as np
import jax
import jax.numpy as jnp
from jax import lax
from jax.experimental import pallas as pl
from jax.experimental.pallas import tpu as pltpu

GRID_W = 64
N_FOURIER_GROUPS = 4
FOURIER_GROUP_DIM = 128
FOURIER_WIDTH = N_FOURIER_GROUPS * FOURIER_GROUP_DIM
NA_HEADS = 8
HEAD_DIM = 64
NA_WIDTH = NA_HEADS * HEAD_DIM
WIN_ROWS = 8
WIN_COLS = 16
ROPE_THETA = 10000.0
NORM_EPS = 1e-6
N_MOD = 6
MASK_VALUE = -1e30
LOG2_E = 1.4426950408889634
SCORE_LOOKAHEAD = 2
ROW_COLS = 4

N_FREQ = HEAD_DIM // 4

LANES = 128
SUBLANES = 8
VMEM_LIMIT_BYTES = 48 * 1024 * 1024

HALF_LANES = LANES // 2
HEAD_LANES = HALF_LANES // (LANES // HEAD_DIM)

BF16 = jnp.bfloat16
F32 = jnp.float32


def _dot(a, b):
    return jnp.dot(a, b, preferred_element_type=F32)


def _dot_nt(a, b):
    return lax.dot_general(a, b, (((1,), (1,)), ((), ())), preferred_element_type=F32)


def _params(*semantics):
    return pltpu.CompilerParams(dimension_semantics=semantics,
                                vmem_limit_bytes=VMEM_LIMIT_BYTES)


def _rms_mod(x, g, shift, scale):
    ms = jnp.mean(x * x, axis=-1, keepdims=True)
    return (x * lax.rsqrt(ms + NORM_EPS) * g) * (1.0 + scale) + shift


def _head_norm(a, gain, head_mean):
    half = NA_WIDTH // 2
    sq = (a * a).astype(BF16)
    ms = jnp.concatenate([_dot(sq[:, :half], head_mean), _dot(sq[:, half:], head_mean)], axis=1)
    return a * lax.rsqrt(ms + NORM_EPS) * gain


def _ada_kernel(cond_ref, w_ref, b_ref, o_ref):
    cond = cond_ref[...]
    s = cond * jax.nn.sigmoid(cond)
    s_hi = s.astype(BF16)
    s_lo = (s - s_hi.astype(F32)).astype(BF16)
    w = w_ref[...]
    w_hi = w.astype(BF16)
    w_lo = (w - w_hi.astype(F32)).astype(BF16)
    o_ref[...] = _dot(s_hi, w_hi) + _dot(s_hi, w_lo) + _dot(s_lo, w_hi) + b_ref[...]


def _ada(cond, w, b):
    rows, d = cond.shape
    n = w.shape[1]
    tn = 1536
    return pl.pallas_call(
        _ada_kernel,
        grid=(n // tn,),
        in_specs=[pl.BlockSpec((rows, d), lambda j: (0, 0)),
                  pl.BlockSpec((d, tn), lambda j: (0, j)),
                  pl.BlockSpec((1, tn), lambda j: (0, j))],
        out_specs=pl.BlockSpec((rows, tn), lambda j: (0, j)),
        out_shape=jax.ShapeDtypeStruct((rows, n), F32),
        compiler_params=_params("arbitrary"),
        name="ada",
    )(cond, w, b)


def _w_in_kernel(w_ref, p_ref, o_ref):
    wb = w_ref[...].astype(BF16)
    v_col = FOURIER_WIDTH + 2 * NA_WIDTH
    o_ref[:, :FOURIER_WIDTH] = wb[:, :FOURIER_WIDTH]
    o_ref[:, v_col:] = wb[:, v_col:]
    for col in (FOURIER_WIDTH, FOURIER_WIDTH + NA_WIDTH):
        o_ref[:, col:col + NA_WIDTH] = _dot(wb[:, col:col + NA_WIDTH], p_ref[...]).astype(BF16)


def _w_in_bf16(w_in, perm_matrix, tr):
    d, n = w_in.shape
    return pl.pallas_call(
        _w_in_kernel,
        grid=(d // tr,),
        in_specs=[pl.BlockSpec((tr, n), lambda i: (i, 0)),
                  pl.BlockSpec(perm_matrix.shape, lambda i: (0, 0))],
        out_specs=pl.BlockSpec((tr, n), lambda i: (i, 0)),
        out_shape=jax.ShapeDtypeStruct((d, n), BF16),
        compiler_params=_params("arbitrary"),
        name="w_in",
    )(w_in, perm_matrix)


def _ctx_kv_kernel(ctx_ref, mod_ref, g1_ref, w_ref, kg_ref, hm_ref, k_ref, v_ref):
    m = mod_ref[...]
    h = _rms_mod(ctx_ref[0], g1_ref[...], m[0:1], m[1:2])
    p = _dot(h.astype(BF16), w_ref[...])
    k_ref[0] = _head_norm(p[:, :NA_WIDTH], kg_ref[...], hm_ref[...]).astype(BF16)
    v_ref[0] = p[:, NA_WIDTH:].astype(BF16)


def _ctx_kv(ctx, mod_ctx, g1, w_in, kg, head_mean):
    b, c, d = ctx.shape
    kv_cols = 2 * NA_WIDTH
    out = jax.ShapeDtypeStruct((b, c, NA_WIDTH), BF16)
    return pl.pallas_call(
        _ctx_kv_kernel,
        grid=(b,),
        in_specs=[pl.BlockSpec((1, c, d), lambda i: (i, 0, 0)),
                  pl.BlockSpec((8, d), lambda i: (0, 0)),
                  pl.BlockSpec((1, d), lambda i: (0, 0)),
                  pl.BlockSpec((d, kv_cols), lambda i: (0, (w_in.shape[1] - kv_cols) // kv_cols)),
                  pl.BlockSpec((1, NA_WIDTH), lambda i: (0, 0)),
                  pl.BlockSpec(head_mean.shape, lambda i: (0, 0))],
        out_specs=[pl.BlockSpec((1, c, NA_WIDTH), lambda i: (i, 0, 0))] * 2,
        out_shape=[out, out],
        compiler_params=_params("arbitrary"),
        name="ctx_kv",
    )(ctx, mod_ctx, g1, w_in, kg, head_mean)


def _rope(a, cos, sin_signed):
    return a * cos + pltpu.roll(a, HALF_LANES, 1) * sin_signed


def _proj_kernel(x_ref, mod_ref, g1_ref, w_ref, qg_ref, kg_ref, hm_ref, cos_ref, sin_ref,
                 f_ref, q_ref, k_ref, v_ref, *, chunk):
    m = mod_ref[0]
    hm = hm_ref[...]
    col_f, col_q, col_k, col_v = (0, FOURIER_WIDTH, FOURIER_WIDTH + NA_WIDTH,
                                  FOURIER_WIDTH + 2 * NA_WIDTH)

    def project(c):
        rows = slice(c * chunk, (c + 1) * chunk)
        h = _rms_mod(x_ref[0, rows, :], g1_ref[...], m[0:1], m[1:2]).astype(BF16)
        return tuple(_dot(h, w_ref[:, col:col + NA_WIDTH]) for col in (col_q, col_k, col_f, col_v))

    def finish(c, pq, pk, pf, pv):
        rows = slice(c * chunk, (c + 1) * chunk)
        cos = cos_ref[rows, :]
        sin = sin_ref[rows, :]
        for p, gain_ref, dst in ((pq, qg_ref, q_ref), (pk, kg_ref, k_ref)):
            a = _head_norm(p, gain_ref[...], hm)
            blocks = [_rope(a[:, j:j + LANES], cos, sin) for j in range(0, NA_WIDTH, LANES)]
            dst[0, rows, :] = jnp.concatenate(blocks, axis=1).astype(BF16)
        f_ref[0, rows, :] = pf.astype(BF16)
        v_ref[0, rows, :] = pv.astype(BF16)

    n_chunks = x_ref.shape[1] // chunk
    pending = project(0)
    for c in range(n_chunks):
        upcoming = project(c + 1) if c + 1 < n_chunks else None
        finish(c, *pending)
        pending = upcoming


def _proj(x, mod, g1, w_in, qg, kg, head_mean, cos, sin, tm):
    b, s, d = x.shape
    out = jax.ShapeDtypeStruct((b, s, NA_WIDTH), BF16)
    tok = lambda t, i: (i, t, 0)
    const = lambda t, i: (0, 0)
    return pl.pallas_call(
        functools.partial(_proj_kernel, chunk=128),
        grid=(s // tm, b),
        in_specs=[pl.BlockSpec((1, tm, d), tok),
                  pl.BlockSpec((1, 8, d), lambda t, i: (i, 0, 0)),
                  pl.BlockSpec((1, d), const),
                  pl.BlockSpec(w_in.shape, const),
                  pl.BlockSpec((1, NA_WIDTH), const),
                  pl.BlockSpec((1, NA_WIDTH), const),
                  pl.BlockSpec(head_mean.shape, const),
                  pl.BlockSpec((tm, cos.shape[1]), lambda t, i: (t, 0)),
                  pl.BlockSpec((tm, sin.shape[1]), lambda t, i: (t, 0))],
        out_specs=[pl.BlockSpec((1, tm, NA_WIDTH), tok)] * 4,
        out_shape=[out] * 4,
        compiler_params=_params("arbitrary", "arbitrary"),
        name="proj",
    )(x, mod, g1, w_in, qg, kg, head_mean, cos, sin)


def _pitch(n):
    tiles = -(-n // SUBLANES)
    return SUBLANES * (tiles + 1 - tiles % 2)


def _fourier_kernel(*refs, rows, b_chunk, n_cast):
    u_ref, cs_ref, w1_ref, g_ref = refs[:4]
    cast_src = refs[4:4 + n_cast]
    y_ref = refs[4 + n_cast]
    cast_dst = refs[5 + n_cast:5 + 2 * n_cast]
    ut_ref, z_ref, tr_ref, ti_ref = refs[5 + 2 * n_cast:]
    gd = FOURIER_GROUP_DIM
    w = GRID_W
    row_pitch = _pitch(rows)

    def reorder(a, carry):
        base = pl.multiple_of(a * w, w)
        ut_ref[pl.ds(a, w, stride=row_pitch), :] = u_ref[0, pl.ds(base, w), :].astype(F32)
        return carry

    def column(b):
        return ut_ref[pl.ds(pl.multiple_of(b * row_pitch, SUBLANES), rows), :].astype(BF16)

    def channel_stage(i, carry):
        b0 = i * b_chunk
        lhs = jnp.concatenate(
            [jnp.concatenate([column(b0 + 2 * p), column(b0 + 2 * p + 1)], axis=1)
             for p in range(b_chunk // 2)], axis=0)
        zc = _dot(lhs, cs_ref[...])
        for t in range(b_chunk):
            block = zc[(t // 2) * rows:(t // 2 + 1) * rows, (t % 2) * 2 * gd:(t % 2 + 1) * 2 * gd]
            z_ref[b0 + t, :rows] = block[:, :gd].astype(BF16)
            z_ref[b0 + t, rows:] = block[:, gd:].astype(BF16)
        return carry

    def row_stage(i, carry):
        t2 = _dot(w1_ref[...],
                  jnp.concatenate([z_ref[ROW_COLS * i + t] for t in range(ROW_COLS)], axis=1))
        for t in range(ROW_COLS):
            b = ROW_COLS * i + t
            for blk in range(rows // SUBLANES):
                dst = pl.multiple_of((blk * w + b) * SUBLANES, SUBLANES)
                lo = blk * SUBLANES
                tr_ref[pl.ds(dst, SUBLANES), :] = t2[lo:lo + SUBLANES, t * gd:(t + 1) * gd]
                ti_ref[pl.ds(dst, SUBLANES), :] = t2[rows + lo:rows + lo + SUBLANES,
                                                     t * gd:(t + 1) * gd]
        return carry

    def col_stage(blk, carry):
        for r_in in range(SUBLANES):
            ka = blk * SUBLANES + r_in
            start = blk * (w * SUBLANES) + r_in
            rhs = jnp.concatenate([tr_ref[pl.ds(start, w, stride=SUBLANES), :],
                                   ti_ref[pl.ds(start, w, stride=SUBLANES), :]], axis=0)
            y = _dot(g_ref[ka], rhs.astype(BF16))
            for kblk in range(w // SUBLANES):
                dst = pl.multiple_of((kblk * rows + ka) * SUBLANES, SUBLANES)
                ut_ref[pl.ds(dst, SUBLANES), :] = y[kblk * SUBLANES:(kblk + 1) * SUBLANES]
        return carry

    def emit(kblk, carry):
        for r_in in range(SUBLANES):
            kb = kblk * SUBLANES + r_in
            start = kblk * (rows * SUBLANES) + r_in
            dst = pl.multiple_of(kb * rows, rows)
            y_ref[0, pl.ds(dst, rows), :] = (
                ut_ref[pl.ds(start, rows, stride=SUBLANES), :].astype(BF16))
        return carry

    lax.fori_loop(0, rows, reorder, 0, unroll=8)
    for src, dst in zip(cast_src, cast_dst):
        dst[...] = src[...].astype(BF16)
    for i in range(w // b_chunk):
        channel_stage(i, 0)
    for i in range(w // ROW_COLS):
        row_stage(i, 0)
    for blk in range(rows // SUBLANES):
        col_stage(blk, 0)
    for kblk in range(w // SUBLANES):
        emit(kblk, 0)


def _fourier(f, cs, w1, g2, weights):
    b, s, width = f.shape
    rows = s // GRID_W
    gd = FOURIER_GROUP_DIM
    n_groups = width // gd
    n_steps = b * n_groups
    slab = pl.BlockSpec((1, s, gd), lambda i, g: (i, 0, g))
    const2 = lambda i, g: (0, 0)
    slices = [pl.BlockSpec((wt.shape[0] // n_steps, wt.shape[1]),
                           lambda i, g: (i * n_groups + g, 0)) for wt in weights]
    assert all(wt.shape[0] % (n_steps * 2 * SUBLANES) == 0 for wt in weights)
    out = pl.pallas_call(
        functools.partial(_fourier_kernel, rows=rows, b_chunk=8, n_cast=len(weights)),
        grid=(b, n_groups),
        in_specs=[slab,
                  pl.BlockSpec(cs.shape, const2),
                  pl.BlockSpec(w1.shape, const2),
                  pl.BlockSpec(g2.shape, lambda i, g: (0, 0, 0))] + slices,
        out_specs=[slab] + slices,
        out_shape=[jax.ShapeDtypeStruct(f.shape, BF16)]
        + [jax.ShapeDtypeStruct(wt.shape, BF16) for wt in weights],
        scratch_shapes=[pltpu.VMEM((GRID_W * _pitch(rows), gd), F32),
                        pltpu.VMEM((GRID_W, 2 * rows, gd), BF16),
                        pltpu.VMEM((rows * GRID_W, gd), F32),
                        pltpu.VMEM((rows * GRID_W, gd), F32)],
        compiler_params=_params("arbitrary", "arbitrary"),
        name="fourier",
    )(f, cs, w1, g2, *weights)
    return out[0], out[1:]


def _fold_lanes(op, *arrays):
    blocks = [a[:, j:j + LANES] for a in arrays for j in range(0, a.shape[1], LANES)]
    while len(blocks) > 1:
        blocks = [op(blocks[i], blocks[i + 1]) if i + 1 < len(blocks) else blocks[i]
                  for i in range(0, len(blocks), 2)]
    return blocks[0]


def _build_bias(rpb_ref, bias_ref):
    n_dr = 2 * WIN_ROWS - 1
    col = lax.broadcasted_iota(jnp.int32, (GRID_W, LANES), 0)
    lane = lax.broadcasted_iota(jnp.int32, (GRID_W, LANES), 1)
    kcol = lane % GRID_W
    col_start = jnp.clip(col - WIN_COLS // 2, 0, GRID_W - WIN_COLS)
    valid = (kcol >= col_start) & (kcol < col_start + WIN_COLS)
    low = lane < GRID_W

    def per_head(h, carry):
        pair = h // 2
        row0 = pl.multiple_of((h % 2) * GRID_W, GRID_W)
        at_low, at_high = [], []
        for dr in range(n_dr):
            p = jnp.broadcast_to(rpb_ref[pl.ds(h * n_dr + dr, 1), :], (GRID_W, LANES))
            at_low.append(pltpu.roll(p, LANES - GRID_W + 1, 1, stride=1, stride_axis=0))
            at_high.append(pltpu.roll(p, LANES - 2 * GRID_W + 1, 1, stride=1, stride_axis=0))
        for variant in range(WIN_ROWS):
            for jp in range(WIN_ROWS // 2):
                dr = 2 * jp - variant + WIN_ROWS - 1
                block = jnp.where(valid, jnp.where(low, at_low[dr], at_high[dr + 1]), MASK_VALUE)
                bias_ref[variant, pair, pl.ds(row0, GRID_W), jp * LANES:(jp + 1) * LANES] = block
        return carry

    lax.fori_loop(0, NA_HEADS, per_head, 0)


def _attn_kernel(q_ref, k_ref, v_ref, kc_ref, vc_ref, rpb_ref, o_ref, bias_ref, *,
                 rows_per_step):
    @pl.when((pl.program_id(0) == 0) & (pl.program_id(1) == 0))
    def _():
        _build_bias(rpb_ref, bias_ref)

    n_rows = k_ref.shape[1] // GRID_W
    n_loc = WIN_ROWS * GRID_W
    lane = lax.broadcasted_iota(jnp.int32, (GRID_W, LANES), 1)
    low = lane < HEAD_DIM
    first = (lane % HALF_LANES) < HEAD_LANES
    units = [(rr, pair) for rr in range(rows_per_step) for pair in range(NA_WIDTH // LANES)]

    def window(rr):
        r = pl.program_id(1) * rows_per_step + rr
        row_start = jnp.clip(r - WIN_ROWS // 2, 0, n_rows - WIN_ROWS)
        return r - row_start, pl.multiple_of(row_start * GRID_W, GRID_W)

    def scores(rr, pair):
        variant, start = window(rr)
        sl = slice(pair * LANES, (pair + 1) * LANES)
        qp = q_ref[0, rr * GRID_W:(rr + 1) * GRID_W, sl]
        zero = jnp.zeros_like(qp)
        q2 = jnp.concatenate([jnp.where(first, qp, zero), jnp.where(first, zero, qp)], axis=0)
        s_loc = _dot_nt(q2, k_ref[0, pl.ds(start, n_loc), sl]) + bias_ref[variant, pair]
        s_ctx = _dot_nt(q2, kc_ref[0, :, sl])
        return s_loc, s_ctx

    def finish(rr, pair, s_loc, s_ctx):
        _, start = window(rr)
        sl = slice(pair * LANES, (pair + 1) * LANES)
        m = jnp.max(_fold_lanes(jnp.maximum, s_loc, s_ctx), axis=-1, keepdims=True)
        p_loc = jnp.exp2(s_loc - m)
        p_ctx = jnp.exp2(s_ctx - m)
        denom = jnp.sum(_fold_lanes(jnp.add, p_loc, p_ctx), axis=-1, keepdims=True)
        o = (_dot(p_loc.astype(BF16), v_ref[0, pl.ds(start, n_loc), sl])
             + _dot(p_ctx.astype(BF16), vc_ref[0, :, sl]))
        o = o / denom
        o_ref[0, rr * GRID_W:(rr + 1) * GRID_W, sl] = (
            jnp.where(low, o[:GRID_W], o[GRID_W:]).astype(BF16))

    pending = [scores(*u) for u in units[:SCORE_LOOKAHEAD]]
    for i, unit in enumerate(units):
        if i + SCORE_LOOKAHEAD < len(units):
            pending.append(scores(*units[i + SCORE_LOOKAHEAD]))
        finish(*unit, *pending.pop(0))


def _attn(q, k, v, kc, vc, rpb_rows, rows_per_step):
    b, s, width = q.shape
    n_rows = s // GRID_W
    c = kc.shape[1]
    tq = rows_per_step * GRID_W
    whole = lambda i, r: (i, 0, 0)
    heads_per_block = LANES // HEAD_DIM
    bias_shape = (WIN_ROWS, NA_HEADS // heads_per_block, heads_per_block * GRID_W,
                  WIN_ROWS * GRID_W)
    return pl.pallas_call(
        functools.partial(_attn_kernel, rows_per_step=rows_per_step),
        grid=(b, n_rows // rows_per_step),
        in_specs=[pl.BlockSpec((1, tq, width), lambda i, r: (i, r, 0)),
                  pl.BlockSpec((1, s, width), whole),
                  pl.BlockSpec((1, s, width), whole),
                  pl.BlockSpec((1, c, width), whole),
                  pl.BlockSpec((1, c, width), whole),
                  pl.BlockSpec(rpb_rows.shape, lambda i, r: (0, 0))],
        out_specs=pl.BlockSpec((1, tq, width), lambda i, r: (i, r, 0)),
        out_shape=jax.ShapeDtypeStruct((b, s, width), BF16),
        scratch_shapes=[pltpu.VMEM(bias_shape, F32)],
        compiler_params=_params("arbitrary", "arbitrary"),
        name="attn",
    )(q, k, v, kc, vc, rpb_rows)


def _merge_kernel(x_ref, ya_ref, yf_ref, mod_ref, g1_ref, wg_ref, wfo_ref, wao_ref, wo_ref,
                  o_ref, *, chunk):
    d = g1_ref.shape[1]
    m = mod_ref[0]

    def branches(c):
        rows = slice(c * chunk, (c + 1) * chunk)
        h = _rms_mod(x_ref[0, rows, :], g1_ref[...], m[0:1], m[1:2]).astype(BF16)
        return (_dot(h, wg_ref[...]), _dot(yf_ref[0, rows, :], wfo_ref[...]),
                _dot(ya_ref[0, rows, :], wao_ref[...]))

    def finish(c, gate_logits, four, attn):
        rows = slice(c * chunk, (c + 1) * chunk)
        gates = jax.nn.sigmoid(gate_logits)
        merged = gates[:, :d] * four + gates[:, d:] * attn
        o_ref[0, rows, :] = x_ref[0, rows, :] + m[2:3] * _dot(merged.astype(BF16), wo_ref[...])

    n_chunks = x_ref.shape[1] // chunk
    pending = branches(0)
    for c in range(n_chunks):
        upcoming = branches(c + 1) if c + 1 < n_chunks else None
        finish(c, *pending)
        pending = upcoming


def _merge(x, y_attn, y_four, mod, g1, w_gate, w_fo, w_ao, w_o, tm):
    b, s, d = x.shape
    const = lambda i, t: (0, 0)
    tok = lambda i, t: (i, t, 0)
    return pl.pallas_call(
        functools.partial(_merge_kernel, chunk=256),
        grid=(b, s // tm),
        in_specs=[pl.BlockSpec((1, tm, d), tok),
                  pl.BlockSpec((1, tm, NA_WIDTH), tok),
                  pl.BlockSpec((1, tm, FOURIER_WIDTH), tok),
                  pl.BlockSpec((1, 8, d), lambda i, t: (i, 0, 0)),
                  pl.BlockSpec((1, d), const),
                  pl.BlockSpec(w_gate.shape, const),
                  pl.BlockSpec(w_fo.shape, const),
                  pl.BlockSpec(w_ao.shape, const),
                  pl.BlockSpec(w_o.shape, const)],
        out_specs=pl.BlockSpec((1, tm, d), tok),
        out_shape=jax.ShapeDtypeStruct(x.shape, F32),
        compiler_params=_params("arbitrary", "arbitrary"),
        name="merge",
    )(x, y_attn, y_four, mod, g1, w_gate, w_fo, w_ao, w_o)


def _mlp_kernel(x_ref, mod_ref, g2_ref, w1_ref, w2_ref, o_ref, *, ff_chunk):
    x = x_ref[0]
    m = mod_ref[0]
    h = _rms_mod(x, g2_ref[...], m[3:4], m[4:5]).astype(BF16)
    acc = jnp.zeros(x.shape, F32)
    for c in range(w1_ref.shape[1] // ff_chunk):
        a = jnp.maximum(_dot(h, w1_ref[:, c * ff_chunk:(c + 1) * ff_chunk]), 0.0)
        acc = acc + _dot((a * a).astype(BF16), w2_ref[c * ff_chunk:(c + 1) * ff_chunk, :])
    o_ref[0] = x + m[5:6] * acc


def _mlp(x, mod, g2, w1, w2, tm, ff_chunk):
    b, s, d = x.shape
    const = lambda i, t: (0, 0)
    return pl.pallas_call(
        functools.partial(_mlp_kernel, ff_chunk=ff_chunk),
        grid=(b, s // tm),
        in_specs=[pl.BlockSpec((1, tm, d), lambda i, t: (i, t, 0)),
                  pl.BlockSpec((1, 8, d), lambda i, t: (i, 0, 0)),
                  pl.BlockSpec((1, d), const),
                  pl.BlockSpec(w1.shape, const, pipeline_mode=pl.Buffered(1)),
                  pl.BlockSpec(w2.shape, const, pipeline_mode=pl.Buffered(1))],
        out_specs=pl.BlockSpec((1, tm, d), lambda i, t: (i, t, 0)),
        out_shape=jax.ShapeDtypeStruct(x.shape, F32),
        compiler_params=_params("arbitrary", "arbitrary"),
        name="mlp",
    )(x, mod, g2, w1, w2)


def _dft_tables(rows):
    n = rows * GRID_W
    gd = FOURIER_GROUP_DIM
    c = np.arange(gd)
    ang = 2.0 * np.pi * np.outer(c, c) / gd
    cs = np.concatenate([np.cos(ang), np.sin(ang)], axis=1) * 2.0 ** -3
    cs = np.kron(np.eye(2), cs)
    a = np.arange(rows)
    ang = 2.0 * np.pi * np.outer(a, a) / rows
    c1, s1 = np.cos(ang), np.sin(ang)
    w1 = np.block([[c1, -s1], [s1, c1]]) * 2.0 ** -4
    b = np.arange(GRID_W)
    k = a[:, None] + rows * b[None, :]
    ang = 2.0 * np.pi * k[:, :, None] * b[None, None, :] / n
    g2 = np.concatenate([np.cos(ang), -np.sin(ang)], axis=2) * 2.0 ** -3
    assert gd * rows * GRID_W == 2 ** 20
    return tuple(jnp.asarray(m, F32).astype(BF16) for m in (cs, w1, g2))


def _qk_lane_perm():
    lane = np.arange(NA_WIDTH)
    pair, in_block = lane // LANES, lane % LANES
    side, in_side = in_block // HALF_LANES, in_block % HALF_LANES
    head = pair * (LANES // HEAD_DIM) + in_side // HEAD_LANES
    axis, freq = (in_side % HEAD_LANES) // N_FREQ, in_side % N_FREQ
    return head * HEAD_DIM + axis * (2 * N_FREQ) + side * N_FREQ + freq


def _gain_lanes(g):
    sides = g.reshape(2, 2, N_FREQ).transpose(1, 0, 2).reshape(2, HEAD_LANES)
    block = jnp.concatenate([jnp.tile(sides[0], LANES // HEAD_DIM),
                             jnp.tile(sides[1], LANES // HEAD_DIM)])
    return jnp.tile(block, NA_WIDTH // LANES)[None, :]


def _rope_tables(seq):
    t = np.arange(seq)
    axis_dim = 2 * N_FREQ
    inv_freq = ROPE_THETA ** (-np.arange(0, axis_dim, 2, dtype=np.float64) / axis_dim)
    ang_r = (t // GRID_W)[:, None] * inv_freq
    ang_c = (t % GRID_W)[:, None] * inv_freq
    ang = np.tile(np.concatenate([ang_r, ang_c], axis=1), (1, LANES // HEAD_LANES))
    sign = np.repeat([-1.0, 1.0], HALF_LANES)
    return jnp.asarray(np.cos(ang), F32), jnp.asarray(np.sin(ang) * sign, F32)


def _rpb_rows(rpb):
    left = GRID_W - WIN_COLS
    padded = jnp.pad(rpb * LOG2_E, ((0, 0), (0, 0), (left, LANES - left - rpb.shape[-1])))
    return padded.reshape(rpb.shape[0] * rpb.shape[1], LANES)


def kernel(x, c, ctx, c_ctx, w_ada, b_ada, norm1_g, norm2_g, w_in, q_norm_g, k_norm_g, rpb,
           w_branch_gate, w_fourier_out, w_attn_out, w_out, w_mlp1, w_mlp2):
    b, s, d = x.shape
    assert w_ada.shape[0] == 1, "single-layer block only"
    n_rows = s // GRID_W
    assert n_rows >= WIN_ROWS and b < 8

    cond = jnp.zeros((8, d), F32).at[:b].set(c).at[b].set(c_ctx)
    mods = _ada(cond, w_ada[0], b_ada)
    mod = jnp.pad(mods[:b].reshape(b, N_MOD, d), ((0, 0), (0, 8 - N_MOD), (0, 0)))
    mod_ctx = jnp.pad(mods[b].reshape(N_MOD, d), ((0, 8 - N_MOD), (0, 0)))

    perm = _qk_lane_perm()
    head_id = perm[:NA_WIDTH // 2] // HEAD_DIM
    head_mean = jnp.asarray((head_id[:, None] == head_id[None, :]) / HEAD_DIM, F32).astype(BF16)
    perm_matrix = jnp.asarray(np.arange(NA_WIDTH)[:, None] == perm[None, :], F32).astype(BF16)
    qg = _gain_lanes(q_norm_g[0]) * (HEAD_DIM ** -0.5 * LOG2_E)
    kg = _gain_lanes(k_norm_g[0])
    cos, sin = _rope_tables(s)
    cs, w1, g2 = _dft_tables(n_rows)
    w_in_b = _w_in_bf16(w_in[0], perm_matrix, tr=256)

    kc, vc = _ctx_kv(ctx, mod_ctx, norm1_g, w_in_b, kg, head_mean)
    f, q, k, v = _proj(x, mod, norm1_g, w_in_b, qg, kg, head_mean, cos, sin, tm=1024)
    y_four, (w_gate_b, w_fo_b, w_ao_b, w_o_b, w_mlp1_b, w_mlp2_b) = _fourier(
        f, cs, w1, g2, [w_branch_gate[0], w_fourier_out[0], w_attn_out[0], w_out[0],
                        w_mlp1[0], w_mlp2[0]])
    y_attn = _attn(q, k, v, kc, vc, _rpb_rows(rpb[0]), rows_per_step=8)
    x1 = _merge(x, y_attn, y_four, mod, norm1_g, w_gate_b, w_fo_b, w_ao_b, w_o_b, tm=1024)
    return _mlp(x1, mod, norm2_g, w_mlp1_b, w_mlp2_b, tm=1024, ff_chunk=1024)
```

```python
import functools

import numpy as np
import jax
import jax.numpy as jnp
from jax import lax
from jax.experimental import pallas as pl
from jax.experimental.pallas import tpu as pltpu

GRID_W = 64
N_FOURIER_GROUPS = 4
FOURIER_GROUP_DIM = 128
FOURIER_WIDTH = N_FOURIER_GROUPS * FOURIER_GROUP_DIM
NA_HEADS = 8
HEAD_DIM = 64
NA_WIDTH = NA_HEADS * HEAD_DIM
WIN_ROWS = 8
WIN_COLS = 16
ROPE_THETA = 10000.0
NORM_EPS = 1e-6
N_MOD = 6
MASK_VALUE = -1e30
LOG2_E = 1.4426950408889634
SCORE_LOOKAHEAD = 2
ROW_COLS = 4

N_FREQ = HEAD_DIM // 4

LANES = 128
SUBLANES = 8
VMEM_LIMIT_BYTES = 48 * 1024 * 1024

HALF_LANES = LANES // 2
HEAD_LANES = HALF_LANES // (LANES // HEAD_DIM)

BF16 = jnp.bfloat16
F32 = jnp.float32


def _dot(a, b):
    return jnp.dot(a, b, preferred_element_type=F32)


def _dot_nt(a, b):
    return lax.dot_general(a, b, (((1,), (1,)), ((), ())), preferred_element_type=F32)


def _params(*semantics):
    return pltpu.CompilerParams(dimension_semantics=semantics,
                                vmem_limit_bytes=VMEM_LIMIT_BYTES)


def _rms_mod(x, g, shift, scale):
    ms = jnp.mean(x * x, axis=-1, keepdims=True)
    return (x * lax.rsqrt(ms + NORM_EPS) * g) * (1.0 + scale) + shift


def _head_norm(a, gain, head_mean):
    half = NA_WIDTH // 2
    sq = (a * a).astype(BF16)
    ms = jnp.concatenate([_dot(sq[:, :half], head_mean), _dot(sq[:, half:], head_mean)], axis=1)
    return a * lax.rsqrt(ms + NORM_EPS) * gain


def _ada_kernel(cond_ref, w_ref, b_ref, o_ref):
    cond = cond_ref[...]
    s = cond * jax.nn.sigmoid(cond)
    s_hi = s.astype(BF16)
    s_lo = (s - s_hi.astype(F32)).astype(BF16)
    w = w_ref[...]
    w_hi = w.astype(BF16)
    w_lo = (w - w_hi.astype(F32)).astype(BF16)
    o_ref[...] = _dot(s_hi, w_hi) + _dot(s_hi, w_lo) + _dot(s_lo, w_hi) + b_ref[...]


def _ada(cond, w, b):
    rows, d = cond.shape
    n = w.shape[1]
    tn = 1536
    return pl.pallas_call(
        _ada_kernel,
        grid=(n // tn,),
        in_specs=[pl.BlockSpec((rows, d), lambda j: (0, 0)),
                  pl.BlockSpec((d, tn), lambda j: (0, j)),
                  pl.BlockSpec((1, tn), lambda j: (0, j))],
        out_specs=pl.BlockSpec((rows, tn), lambda j: (0, j)),
        out_shape=jax.ShapeDtypeStruct((rows, n), F32),
        compiler_params=_params("arbitrary"),
        name="ada",
    )(cond, w, b)


def _w_in_kernel(w_ref, p_ref, o_ref):
    wb = w_ref[...].astype(BF16)
    v_col = FOURIER_WIDTH + 2 * NA_WIDTH
    o_ref[:, :FOURIER_WIDTH] = wb[:, :FOURIER_WIDTH]
    o_ref[:, v_col:] = wb[:, v_col:]
    for col in (FOURIER_WIDTH, FOURIER_WIDTH + NA_WIDTH):
        o_ref[:, col:col + NA_WIDTH] = _dot(wb[:, col:col + NA_WIDTH], p_ref[...]).astype(BF16)


def _w_in_bf16(w_in, perm_matrix, tr):
    d, n = w_in.shape
    return pl.pallas_call(
        _w_in_kernel,
        grid=(d // tr,),
        in_specs=[pl.BlockSpec((tr, n), lambda i: (i, 0)),
                  pl.BlockSpec(perm_matrix.shape, lambda i: (0, 0))],
        out_specs=pl.BlockSpec((tr, n), lambda i: (i, 0)),
        out_shape=jax.ShapeDtypeStruct((d, n), BF16),
        compiler_params=_params("arbitrary"),
        name="w_in",
    )(w_in, perm_matrix)


def _ctx_kv_kernel(ctx_ref, mod_ref, g1_ref, w_ref, kg_ref, hm_ref, k_ref, v_ref):
    m = mod_ref[...]
    h = _rms_mod(ctx_ref[0], g1_ref[...], m[0:1], m[1:2])
    p = _dot(h.astype(BF16), w_ref[...])
    k_ref[0] = _head_norm(p[:, :NA_WIDTH], kg_ref[...], hm_ref[...]).astype(BF16)
    v_ref[0] = p[:, NA_WIDTH:].astype(BF16)


def _ctx_kv(ctx, mod_ctx, g1, w_in, kg, head_mean):
    b, c, d = ctx.shape
    kv_cols = 2 * NA_WIDTH
    out = jax.ShapeDtypeStruct((b, c, NA_WIDTH), BF16)
    return pl.pallas_call(
        _ctx_kv_kernel,
        grid=(b,),
        in_specs=[pl.BlockSpec((1, c, d), lambda i: (i, 0, 0)),
                  pl.BlockSpec((8, d), lambda i: (0, 0)),
                  pl.BlockSpec((1, d), lambda i: (0, 0)),
                  pl.BlockSpec((d, kv_cols), lambda i: (0, (w_in.shape[1] - kv_cols) // kv_cols)),
                  pl.BlockSpec((1, NA_WIDTH), lambda i: (0, 0)),
                  pl.BlockSpec(head_mean.shape, lambda i: (0, 0))],
        out_specs=[pl.BlockSpec((1, c, NA_WIDTH), lambda i: (i, 0, 0))] * 2,
        out_shape=[out, out],
        compiler_params=_params("arbitrary"),
        name="ctx_kv",
    )(ctx, mod_ctx, g1, w_in, kg, head_mean)


def _rope(a, cos, sin_signed):
    return a * cos + pltpu.roll(a, HALF_LANES, 1) * sin_signed


def _proj_kernel(x_ref, mod_ref, g1_ref, w_ref, qg_ref, kg_ref, hm_ref, cos_ref, sin_ref,
                 f_ref, q_ref, k_ref, v_ref, *, chunk):
    m = mod_ref[0]
    hm = hm_ref[...]
    col_f, col_q, col_k, col_v = (0, FOURIER_WIDTH, FOURIER_WIDTH + NA_WIDTH,
                                  FOURIER_WIDTH + 2 * NA_WIDTH)

    def project(c):
        rows = slice(c * chunk, (c + 1) * chunk)
        h = _rms_mod(x_ref[0, rows, :], g1_ref[...], m[0:1], m[1:2]).astype(BF16)
        return tuple(_dot(h, w_ref[:, col:col + NA_WIDTH]) for col in (col_q, col_k, col_f, col_v))

    def finish(c, pq, pk, pf, pv):
        rows = slice(c * chunk, (c + 1) * chunk)
        cos = cos_ref[rows, :]
        sin = sin_ref[rows, :]
        for p, gain_ref, dst in ((pq, qg_ref, q_ref), (pk, kg_ref, k_ref)):
            a = _head_norm(p, gain_ref[...], hm)
            blocks = [_rope(a[:, j:j + LANES], cos, sin) for j in range(0, NA_WIDTH, LANES)]
            dst[0, rows, :] = jnp.concatenate(blocks, axis=1).astype(BF16)
        f_ref[0, rows, :] = pf.astype(BF16)
        v_ref[0, rows, :] = pv.astype(BF16)

    n_chunks = x_ref.shape[1] // chunk
    pending = project(0)
    for c in range(n_chunks):
        upcoming = project(c + 1) if c + 1 < n_chunks else None
        finish(c, *pending)
        pending = upcoming


def _proj(x, mod, g1, w_in, qg, kg, head_mean, cos, sin, tm):
    b, s, d = x.shape
    out = jax.ShapeDtypeStruct((b, s, NA_WIDTH), BF16)
    tok = lambda t, i: (i, t, 0)
    const = lambda t, i: (0, 0)
    return pl.pallas_call(
        functools.partial(_proj_kernel, chunk=128),
        grid=(s // tm, b),
        in_specs=[pl.BlockSpec((1, tm, d), tok),
                  pl.BlockSpec((1, 8, d), lambda t, i: (i, 0, 0)),
                  pl.BlockSpec((1, d), const),
                  pl.BlockSpec(w_in.shape, const, pipeline_mode=pl.Buffered(1)),
                  pl.BlockSpec((1, NA_WIDTH), const),
                  pl.BlockSpec((1, NA_WIDTH), const),
                  pl.BlockSpec(head_mean.shape, const),
                  pl.BlockSpec((tm, cos.shape[1]), lambda t, i: (t, 0)),
                  pl.BlockSpec((tm, sin.shape[1]), lambda t, i: (t, 0))],
        out_specs=[pl.BlockSpec((1, tm, NA_WIDTH), tok)] * 4,
        out_shape=[out] * 4,
        compiler_params=_params("arbitrary", "arbitrary"),
        name="proj",
    )(x, mod, g1, w_in, qg, kg, head_mean, cos, sin)


def _pitch(n):
    tiles = -(-n // SUBLANES)
    return SUBLANES * (tiles + 1 - tiles % 2)


def _fourier_kernel(*refs, rows, b_chunk, n_cast):
    u_ref, cs_ref, w1_ref, g_ref = refs[:4]
    cast_src = refs[4:4 + n_cast]
    y_ref = refs[4 + n_cast]
    cast_dst = refs[5 + n_cast:5 + 2 * n_cast]
    ut_ref, z_ref, tr_ref, ti_ref = refs[5 + 2 * n_cast:]
    gd = FOURIER_GROUP_DIM
    w = GRID_W
    row_pitch = _pitch(rows)

    def reorder(a, carry):
        base = pl.multiple_of(a * w, w)
        ut_ref[pl.ds(a, w, stride=row_pitch), :] = u_ref[0, pl.ds(base, w), :].astype(F32)
        return carry

    def column(b):
        return ut_ref[pl.ds(pl.multiple_of(b * row_pitch, SUBLANES), rows), :].astype(BF16)

    def channel_stage(i, carry):
        b0 = i * b_chunk
        lhs = jnp.concatenate(
            [jnp.concatenate([column(b0 + 2 * p), column(b0 + 2 * p + 1)], axis=1)
             for p in range(b_chunk // 2)], axis=0)
        zc = _dot(lhs, cs_ref[...])
        for t in range(b_chunk):
            block = zc[(t // 2) * rows:(t // 2 + 1) * rows, (t % 2) * 2 * gd:(t % 2 + 1) * 2 * gd]
            z_ref[b0 + t, :rows] = block[:, :gd].astype(BF16)
            z_ref[b0 + t, rows:] = block[:, gd:].astype(BF16)
        return carry

    def row_stage(i, carry):
        t2 = _dot(w1_ref[...],
                  jnp.concatenate([z_ref[ROW_COLS * i + t] for t in range(ROW_COLS)], axis=1))
        for t in range(ROW_COLS):
            b = ROW_COLS * i + t
            for blk in range(rows // SUBLANES):
                dst = pl.multiple_of((blk * w + b) * SUBLANES, SUBLANES)
                lo = blk * SUBLANES
                tr_ref[pl.ds(dst, SUBLANES), :] = t2[lo:lo + SUBLANES, t * gd:(t + 1) * gd]
                ti_ref[pl.ds(dst, SUBLANES), :] = t2[rows + lo:rows + lo + SUBLANES,
                                                     t * gd:(t + 1) * gd]
        return carry

    def col_stage(blk, carry):
        for r_in in range(SUBLANES):
            ka = blk * SUBLANES + r_in
            start = blk * (w * SUBLANES) + r_in
            rhs = jnp.concatenate([tr_ref[pl.ds(start, w, stride=SUBLANES), :],
                                   ti_ref[pl.ds(start, w, stride=SUBLANES), :]], axis=0)
            y = _dot(g_ref[ka], rhs.astype(BF16))
            for kblk in range(w // SUBLANES):
                dst = pl.multiple_of((kblk * rows + ka) * SUBLANES, SUBLANES)
                ut_ref[pl.ds(dst, SUBLANES), :] = y[kblk * SUBLANES:(kblk + 1) * SUBLANES]
        return carry

    def emit(kblk, carry):
        for r_in in range(SUBLANES):
            kb = kblk * SUBLANES + r_in
            start = kblk * (rows * SUBLANES) + r_in
            dst = pl.multiple_of(kb * rows, rows)
            y_ref[0, pl.ds(dst, rows), :] = (
                ut_ref[pl.ds(start, rows, stride=SUBLANES), :].astype(BF16))
        return carry

    lax.fori_loop(0, rows, reorder, 0, unroll=8)
    for src, dst in zip(cast_src, cast_dst):
        dst[...] = src[...].astype(BF16)
    for i in range(w // b_chunk):
        channel_stage(i, 0)
    for i in range(w // ROW_COLS):
        row_stage(i, 0)
    for blk in range(rows // SUBLANES):
        col_stage(blk, 0)
    for kblk in range(w // SUBLANES):
        emit(kblk, 0)


def _fourier(f, cs, w1, g2, weights):
    b, s, width = f.shape
    rows = s // GRID_W
    gd = FOURIER_GROUP_DIM
    n_groups = width // gd
    n_steps = b * n_groups
    slab = pl.BlockSpec((1, s, gd), lambda i, g: (i, 0, g))
    const2 = lambda i, g: (0, 0)
    slices = [pl.BlockSpec((wt.shape[0] // n_steps, wt.shape[1]),
                           lambda i, g: (i * n_groups + g, 0)) for wt in weights]
    assert all(wt.shape[0] % (n_steps * 2 * SUBLANES) == 0 for wt in weights)
    out = pl.pallas_call(
        functools.partial(_fourier_kernel, rows=rows, b_chunk=8, n_cast=len(weights)),
        grid=(b, n_groups),
        in_specs=[slab,
                  pl.BlockSpec(cs.shape, const2),
                  pl.BlockSpec(w1.shape, const2),
                  pl.BlockSpec(g2.shape, lambda i, g: (0, 0, 0))] + slices,
        out_specs=[slab] + slices,
        out_shape=[jax.ShapeDtypeStruct(f.shape, BF16)]
        + [jax.ShapeDtypeStruct(wt.shape, BF16) for wt in weights],
        scratch_shapes=[pltpu.VMEM((GRID_W * _pitch(rows), gd), F32),
                        pltpu.VMEM((GRID_W, 2 * rows, gd), BF16),
                        pltpu.VMEM((rows * GRID_W, gd), F32),
                        pltpu.VMEM((rows * GRID_W, gd), F32)],
        compiler_params=_params("arbitrary", "arbitrary"),
        name="fourier",
    )(f, cs, w1, g2, *weights)
    return out[0], out[1:]


def _fold_lanes(op, *arrays):
    blocks = [a[:, j:j + LANES] for a in arrays for j in range(0, a.shape[1], LANES)]
    while len(blocks) > 1:
        blocks = [op(blocks[i], blocks[i + 1]) if i + 1 < len(blocks) else blocks[i]
                  for i in range(0, len(blocks), 2)]
    return blocks[0]


def _build_bias(rpb_ref, bias_ref):
    n_dr = 2 * WIN_ROWS - 1
    col = lax.broadcasted_iota(jnp.int32, (GRID_W, LANES), 0)
    lane = lax.broadcasted_iota(jnp.int32, (GRID_W, LANES), 1)
    kcol = lane % GRID_W
    col_start = jnp.clip(col - WIN_COLS // 2, 0, GRID_W - WIN_COLS)
    valid = (kcol >= col_start) & (kcol < col_start + WIN_COLS)
    low = lane < GRID_W

    def per_head(h, carry):
        pair = h // 2
        row0 = pl.multiple_of((h % 2) * GRID_W, GRID_W)
        at_low, at_high = [], []
        for dr in range(n_dr):
            p = jnp.broadcast_to(rpb_ref[pl.ds(h * n_dr + dr, 1), :], (GRID_W, LANES))
            at_low.append(pltpu.roll(p, LANES - GRID_W + 1, 1, stride=1, stride_axis=0))
            at_high.append(pltpu.roll(p, LANES - 2 * GRID_W + 1, 1, stride=1, stride_axis=0))
        for variant in range(WIN_ROWS):
            for jp in range(WIN_ROWS // 2):
                dr = 2 * jp - variant + WIN_ROWS - 1
                block = jnp.where(valid, jnp.where(low, at_low[dr], at_high[dr + 1]), MASK_VALUE)
                bias_ref[variant, pair, pl.ds(row0, GRID_W), jp * LANES:(jp + 1) * LANES] = block
        return carry

    lax.fori_loop(0, NA_HEADS, per_head, 0)


def _attn_kernel(q_ref, k_ref, v_ref, kc_ref, vc_ref, rpb_ref, o_ref, bias_ref, *,
                 rows_per_step):
    @pl.when((pl.program_id(0) == 0) & (pl.program_id(1) == 0))
    def _():
        _build_bias(rpb_ref, bias_ref)

    n_rows = k_ref.shape[1] // GRID_W
    n_loc = WIN_ROWS * GRID_W
    lane = lax.broadcasted_iota(jnp.int32, (GRID_W, LANES), 1)
    low = lane < HEAD_DIM
    first = (lane % HALF_LANES) < HEAD_LANES
    units = [(rr, pair) for rr in range(rows_per_step) for pair in range(NA_WIDTH // LANES)]

    def window(rr):
        r = pl.program_id(1) * rows_per_step + rr
        row_start = jnp.clip(r - WIN_ROWS // 2, 0, n_rows - WIN_ROWS)
        return r - row_start, pl.multiple_of(row_start * GRID_W, GRID_W)

    def scores(rr, pair):
        variant, start = window(rr)
        sl = slice(pair * LANES, (pair + 1) * LANES)
        qp = q_ref[0, rr * GRID_W:(rr + 1) * GRID_W, sl]
        zero = jnp.zeros_like(qp)
        q2 = jnp.concatenate([jnp.where(first, qp, zero), jnp.where(first, zero, qp)], axis=0)
        s_loc = _dot_nt(q2, k_ref[0, pl.ds(start, n_loc), sl]) + bias_ref[variant, pair]
        s_ctx = _dot_nt(q2, kc_ref[0, :, sl])
        return s_loc, s_ctx

    def finish(rr, pair, s_loc, s_ctx):
        _, start = window(rr)
        sl = slice(pair * LANES, (pair + 1) * LANES)
        m = jnp.max(_fold_lanes(jnp.maximum, s_loc, s_ctx), axis=-1, keepdims=True)
        p_loc = jnp.exp2(s_loc - m)
        p_ctx = jnp.exp2(s_ctx - m)
        denom = jnp.sum(_fold_lanes(jnp.add, p_loc, p_ctx), axis=-1, keepdims=True)
        o = (_dot(p_loc.astype(BF16), v_ref[0, pl.ds(start, n_loc), sl])
             + _dot(p_ctx.astype(BF16), vc_ref[0, :, sl]))
        o = o / denom
        o_ref[0, rr * GRID_W:(rr + 1) * GRID_W, sl] = (
            jnp.where(low, o[:GRID_W], o[GRID_W:]).astype(BF16))

    pending = [scores(*u) for u in units[:SCORE_LOOKAHEAD]]
    for i, unit in enumerate(units):
        if i + SCORE_LOOKAHEAD < len(units):
            pending.append(scores(*units[i + SCORE_LOOKAHEAD]))
        finish(*unit, *pending.pop(0))


def _attn(q, k, v, kc, vc, rpb_rows, rows_per_step):
    b, s, width = q.shape
    n_rows = s // GRID_W
    c = kc.shape[1]
    tq = rows_per_step * GRID_W
    whole = lambda i, r: (i, 0, 0)
    heads_per_block = LANES // HEAD_DIM
    bias_shape = (WIN_ROWS, NA_HEADS // heads_per_block, heads_per_block * GRID_W,
                  WIN_ROWS * GRID_W)
    return pl.pallas_call(
        functools.partial(_attn_kernel, rows_per_step=rows_per_step),
        grid=(b, n_rows // rows_per_step),
        in_specs=[pl.BlockSpec((1, tq, width), lambda i, r: (i, r, 0)),
                  pl.BlockSpec((1, s, width), whole),
                  pl.BlockSpec((1, s, width), whole),
                  pl.BlockSpec((1, c, width), whole),
                  pl.BlockSpec((1, c, width), whole),
                  pl.BlockSpec(rpb_rows.shape, lambda i, r: (0, 0))],
        out_specs=pl.BlockSpec((1, tq, width), lambda i, r: (i, r, 0)),
        out_shape=jax.ShapeDtypeStruct((b, s, width), BF16),
        scratch_shapes=[pltpu.VMEM(bias_shape, F32)],
        compiler_params=_params("arbitrary", "arbitrary"),
        name="attn",
    )(q, k, v, kc, vc, rpb_rows)


def _merge_kernel(x_ref, ya_ref, yf_ref, mod_ref, g1_ref, wg_ref, wfo_ref, wao_ref, wo_ref,
                  o_ref, *, chunk):
    d = g1_ref.shape[1]
    m = mod_ref[0]

    def branches(c):
        rows = slice(c * chunk, (c + 1) * chunk)
        h = _rms_mod(x_ref[0, rows, :], g1_ref[...], m[0:1], m[1:2]).astype(BF16)
        return (_dot(h, wg_ref[...]), _dot(yf_ref[0, rows, :], wfo_ref[...]),
                _dot(ya_ref[0, rows, :], wao_ref[...]))

    def finish(c, gate_logits, four, attn):
        rows = slice(c * chunk, (c + 1) * chunk)
        gates = jax.nn.sigmoid(gate_logits)
        merged = gates[:, :d] * four + gates[:, d:] * attn
        o_ref[0, rows, :] = x_ref[0, rows, :] + m[2:3] * _dot(merged.astype(BF16), wo_ref[...])

    n_chunks = x_ref.shape[1] // chunk
    pending = branches(0)
    for c in range(n_chunks):
        upcoming = branches(c + 1) if c + 1 < n_chunks else None
        finish(c, *pending)
        pending = upcoming


def _merge(x, y_attn, y_four, mod, g1, w_gate, w_fo, w_ao, w_o, tm):
    b, s, d = x.shape
    const = lambda i, t: (0, 0)
    tok = lambda i, t: (i, t, 0)
    return pl.pallas_call(
        functools.partial(_merge_kernel, chunk=256),
        grid=(b, s // tm),
        in_specs=[pl.BlockSpec((1, tm, d), tok),
                  pl.BlockSpec((1, tm, NA_WIDTH), tok),
                  pl.BlockSpec((1, tm, FOURIER_WIDTH), tok),
                  pl.BlockSpec((1, 8, d), lambda i, t: (i, 0, 0)),
                  pl.BlockSpec((1, d), const),
                  pl.BlockSpec(w_gate.shape, const, pipeline_mode=pl.Buffered(1)),
                  pl.BlockSpec(w_fo.shape, const, pipeline_mode=pl.Buffered(1)),
                  pl.BlockSpec(w_ao.shape, const, pipeline_mode=pl.Buffered(1)),
                  pl.BlockSpec(w_o.shape, const, pipeline_mode=pl.Buffered(1))],
        out_specs=pl.BlockSpec((1, tm, d), tok),
        out_shape=jax.ShapeDtypeStruct(x.shape, F32),
        compiler_params=_params("arbitrary", "arbitrary"),
        name="merge",
    )(x, y_attn, y_four, mod, g1, w_gate, w_fo, w_ao, w_o)


def _mlp_kernel(x_ref, mod_ref, g2_ref, w1_ref, w2_ref, o_ref, *, ff_chunk):
    x = x_ref[0]
    m = mod_ref[0]
    h = _rms_mod(x, g2_ref[...], m[3:4], m[4:5]).astype(BF16)
    acc = jnp.zeros(x.shape, F32)
    for c in range(w1_ref.shape[1] // ff_chunk):
        a = jnp.maximum(_dot(h, w1_ref[:, c * ff_chunk:(c + 1) * ff_chunk]), 0.0)
        acc = acc + _dot((a * a).astype(BF16), w2_ref[c * ff_chunk:(c + 1) * ff_chunk, :])
    o_ref[0] = x + m[5:6] * acc


def _mlp(x, mod, g2, w1, w2, tm, ff_chunk):
    b, s, d = x.shape
    const = lambda i, t: (0, 0)
    return pl.pallas_call(
        functools.partial(_mlp_kernel, ff_chunk=ff_chunk),
        grid=(b, s // tm),
        in_specs=[pl.BlockSpec((1, tm, d), lambda i, t: (i, t, 0)),
                  pl.BlockSpec((1, 8, d), lambda i, t: (i, 0, 0)),
                  pl.BlockSpec((1, d), const),
                  pl.BlockSpec(w1.shape, const, pipeline_mode=pl.Buffered(1)),
                  pl.BlockSpec(w2.shape, const, pipeline_mode=pl.Buffered(1))],
        out_specs=pl.BlockSpec((1, tm, d), lambda i, t: (i, t, 0)),
        out_shape=jax.ShapeDtypeStruct(x.shape, F32),
        compiler_params=_params("arbitrary", "arbitrary"),
        name="mlp",
    )(x, mod, g2, w1, w2)


def _dft_tables(rows):
    n = rows * GRID_W
    gd = FOURIER_GROUP_DIM
    c = np.arange(gd)
    ang = 2.0 * np.pi * np.outer(c, c) / gd
    cs = np.concatenate([np.cos(ang), np.sin(ang)], axis=1) * 2.0 ** -3
    cs = np.kron(np.eye(2), cs)
    a = np.arange(rows)
    ang = 2.0 * np.pi * np.outer(a, a) / rows
    c1, s1 = np.cos(ang), np.sin(ang)
    w1 = np.block([[c1, -s1], [s1, c1]]) * 2.0 ** -4
    b = np.arange(GRID_W)
    k = a[:, None] + rows * b[None, :]
    ang = 2.0 * np.pi * k[:, :, None] * b[None, None, :] / n
    g2 = np.concatenate([np.cos(ang), -np.sin(ang)], axis=2) * 2.0 ** -3
    assert gd * rows * GRID_W == 2 ** 20
    return tuple(jnp.asarray(m, F32).astype(BF16) for m in (cs, w1, g2))


def _qk_lane_perm():
    lane = np.arange(NA_WIDTH)
    pair, in_block = lane // LANES, lane % LANES
    side, in_side = in_block // HALF_LANES, in_block % HALF_LANES
    head = pair * (LANES // HEAD_DIM) + in_side // HEAD_LANES
    axis, freq = (in_side % HEAD_LANES) // N_FREQ, in_side % N_FREQ
    return head * HEAD_DIM + axis * (2 * N_FREQ) + side * N_FREQ + freq


def _gain_lanes(g):
    sides = g.reshape(2, 2, N_FREQ).transpose(1, 0, 2).reshape(2, HEAD_LANES)
    block = jnp.concatenate([jnp.tile(sides[0], LANES // HEAD_DIM),
                             jnp.tile(sides[1], LANES // HEAD_DIM)])
    return jnp.tile(block, NA_WIDTH // LANES)[None, :]


def _rope_tables(seq):
    t = np.arange(seq)
    axis_dim = 2 * N_FREQ
    inv_freq = ROPE_THETA ** (-np.arange(0, axis_dim, 2, dtype=np.float64) / axis_dim)
    ang_r = (t // GRID_W)[:, None] * inv_freq
    ang_c = (t % GRID_W)[:, None] * inv_freq
    ang = np.tile(np.concatenate([ang_r, ang_c], axis=1), (1, LANES // HEAD_LANES))
    sign = np.repeat([-1.0, 1.0], HALF_LANES)
    return jnp.asarray(np.cos(ang), F32), jnp.asarray(np.sin(ang) * sign, F32)


def _rpb_rows(rpb):
    left = GRID_W - WIN_COLS
    padded = jnp.pad(rpb * LOG2_E, ((0, 0), (0, 0), (left, LANES - left - rpb.shape[-1])))
    return padded.reshape(rpb.shape[0] * rpb.shape[1], LANES)


def kernel(x, c, ctx, c_ctx, w_ada, b_ada, norm1_g, norm2_g, w_in, q_norm_g, k_norm_g, rpb,
           w_branch_gate, w_fourier_out, w_attn_out, w_out, w_mlp1, w_mlp2):
    b, s, d = x.shape
    assert w_ada.shape[0] == 1, "single-layer block only"
    n_rows = s // GRID_W
    assert n_rows >= WIN_ROWS and b < 8

    cond = jnp.zeros((8, d), F32).at[:b].set(c).at[b].set(c_ctx)
    mods = _ada(cond, w_ada[0], b_ada)
    mod = jnp.pad(mods[:b].reshape(b, N_MOD, d), ((0, 0), (0, 8 - N_MOD), (0, 0)))
    mod_ctx = jnp.pad(mods[b].reshape(N_MOD, d), ((0, 8 - N_MOD), (0, 0)))

    perm = _qk_lane_perm()
    head_id = perm[:NA_WIDTH // 2] // HEAD_DIM
    head_mean = jnp.asarray((head_id[:, None] == head_id[None, :]) / HEAD_DIM, F32).astype(BF16)
    perm_matrix = jnp.asarray(np.arange(NA_WIDTH)[:, None] == perm[None, :], F32).astype(BF16)
    qg = _gain_lanes(q_norm_g[0]) * (HEAD_DIM ** -0.5 * LOG2_E)
    kg = _gain_lanes(k_norm_g[0])
    cos, sin = _rope_tables(s)
    cs, w1, g2 = _dft_tables(n_rows)
    w_in_b = _w_in_bf16(w_in[0], perm_matrix, tr=256)

    kc, vc = _ctx_kv(ctx, mod_ctx, norm1_g, w_in_b, kg, head_mean)
    f, q, k, v = _proj(x, mod, norm1_g, w_in_b, qg, kg, head_mean, cos, sin, tm=2048)
    y_four, (w_gate_b, w_fo_b, w_ao_b, w_o_b, w_mlp1_b, w_mlp2_b) = _fourier(
        f, cs, w1, g2, [w_branch_gate[0], w_fourier_out[0], w_attn_out[0], w_out[0],
                        w_mlp1[0], w_mlp2[0]])
    y_attn = _attn(q, k, v, kc, vc, _rpb_rows(rpb[0]), rows_per_step=8)
    x1 = _merge(x, y_attn, y_four, mod, norm1_g, w_gate_b, w_fo_b, w_ao_b, w_o_b, tm=1024)
    return _mlp(x1, mod, norm2_g, w_mlp1_b, w_mlp2_b, tm=1024, ff_chunk=1024)
```

```python
import functools

import numpy as np
import jax
import jax.numpy as jnp
from jax import lax
from jax.experimental import pallas as pl
from jax.experimental.pallas import tpu as pltpu

GRID_W = 64
N_FOURIER_GROUPS = 4
FOURIER_GROUP_DIM = 128
FOURIER_WIDTH = N_FOURIER_GROUPS * FOURIER_GROUP_DIM
NA_HEADS = 8
HEAD_DIM = 64
NA_WIDTH = NA_HEADS * HEAD_DIM
WIN_ROWS = 8
WIN_COLS = 16
ROPE_THETA = 10000.0
NORM_EPS = 1e-6
N_MOD = 6
MASK_VALUE = -1e30
LOG2_E = 1.4426950408889634
SCORE_LOOKAHEAD = 2
ROW_COLS = 4

N_FREQ = HEAD_DIM // 4

LANES = 128
SUBLANES = 8
VMEM_LIMIT_BYTES = 48 * 1024 * 1024

HALF_LANES = LANES // 2
HEAD_LANES = HALF_LANES // (LANES // HEAD_DIM)

BF16 = jnp.bfloat16
F32 = jnp.float32


def _dot(a, b):
    return jnp.dot(a, b, preferred_element_type=F32)


def _dot_nt(a, b):
    return lax.dot_general(a, b, (((1,), (1,)), ((), ())), preferred_element_type=F32)


def _params(*semantics):
    return pltpu.CompilerParams(dimension_semantics=semantics,
                                vmem_limit_bytes=VMEM_LIMIT_BYTES)


def _rms_mod(x, g, shift, scale):
    ms = jnp.mean(x * x, axis=-1, keepdims=True)
    return (x * lax.rsqrt(ms + NORM_EPS) * g) * (1.0 + scale) + shift


def _head_norm(a, gain, head_mean):
    half = NA_WIDTH // 2
    sq = (a * a).astype(BF16)
    ms = jnp.concatenate([_dot(sq[:, :half], head_mean), _dot(sq[:, half:], head_mean)], axis=1)
    return a * lax.rsqrt(ms + NORM_EPS) * gain


def _ada_kernel(cond_ref, w_ref, b_ref, o_ref):
    cond = cond_ref[...]
    s = cond * jax.nn.sigmoid(cond)
    s_hi = s.astype(BF16)
    s_lo = (s - s_hi.astype(F32)).astype(BF16)
    w = w_ref[...]
    w_hi = w.astype(BF16)
    w_lo = (w - w_hi.astype(F32)).astype(BF16)
    o_ref[...] = _dot(s_hi, w_hi) + _dot(s_hi, w_lo) + _dot(s_lo, w_hi) + b_ref[...]


def _ada(cond, w, b):
    rows, d = cond.shape
    n = w.shape[1]
    tn = 1536
    return pl.pallas_call(
        _ada_kernel,
        grid=(n // tn,),
        in_specs=[pl.BlockSpec((rows, d), lambda j: (0, 0)),
                  pl.BlockSpec((d, tn), lambda j: (0, j)),
                  pl.BlockSpec((1, tn), lambda j: (0, j))],
        out_specs=pl.BlockSpec((rows, tn), lambda j: (0, j)),
        out_shape=jax.ShapeDtypeStruct((rows, n), F32),
        compiler_params=_params("arbitrary"),
        name="ada",
    )(cond, w, b)


def _w_in_kernel(w_ref, p_ref, o_ref):
    wb = w_ref[...].astype(BF16)
    v_col = FOURIER_WIDTH + 2 * NA_WIDTH
    o_ref[:, :FOURIER_WIDTH] = wb[:, :FOURIER_WIDTH]
    o_ref[:, v_col:] = wb[:, v_col:]
    for col in (FOURIER_WIDTH, FOURIER_WIDTH + NA_WIDTH):
        o_ref[:, col:col + NA_WIDTH] = _dot(wb[:, col:col + NA_WIDTH], p_ref[...]).astype(BF16)


def _w_in_bf16(w_in, perm_matrix, tr):
    d, n = w_in.shape
    return pl.pallas_call(
        _w_in_kernel,
        grid=(d // tr,),
        in_specs=[pl.BlockSpec((tr, n), lambda i: (i, 0)),
                  pl.BlockSpec(perm_matrix.shape, lambda i: (0, 0))],
        out_specs=pl.BlockSpec((tr, n), lambda i: (i, 0)),
        out_shape=jax.ShapeDtypeStruct((d, n), BF16),
        compiler_params=_params("arbitrary"),
        name="w_in",
    )(w_in, perm_matrix)


def _ctx_kv_kernel(ctx_ref, mod_ref, g1_ref, w_ref, kg_ref, hm_ref, k_ref, v_ref):
    m = mod_ref[...]
    h = _rms_mod(ctx_ref[0], g1_ref[...], m[0:1], m[1:2])
    p = _dot(h.astype(BF16), w_ref[...])
    k_ref[0] = _head_norm(p[:, :NA_WIDTH], kg_ref[...], hm_ref[...]).astype(BF16)
    v_ref[0] = p[:, NA_WIDTH:].astype(BF16)


def _ctx_kv(ctx, mod_ctx, g1, w_in, kg, head_mean):
    b, c, d = ctx.shape
    kv_cols = 2 * NA_WIDTH
    out = jax.ShapeDtypeStruct((b, c, NA_WIDTH), BF16)
    return pl.pallas_call(
        _ctx_kv_kernel,
        grid=(b,),
        in_specs=[pl.BlockSpec((1, c, d), lambda i: (i, 0, 0)),
                  pl.BlockSpec((8, d), lambda i: (0, 0)),
                  pl.BlockSpec((1, d), lambda i: (0, 0)),
                  pl.BlockSpec((d, kv_cols), lambda i: (0, (w_in.shape[1] - kv_cols) // kv_cols)),
                  pl.BlockSpec((1, NA_WIDTH), lambda i: (0, 0)),
                  pl.BlockSpec(head_mean.shape, lambda i: (0, 0))],
        out_specs=[pl.BlockSpec((1, c, NA_WIDTH), lambda i: (i, 0, 0))] * 2,
        out_shape=[out, out],
        compiler_params=_params("arbitrary"),
        name="ctx_kv",
    )(ctx, mod_ctx, g1, w_in, kg, head_mean)


def _rope(a, cos, sin_signed):
    return a * cos + pltpu.roll(a, HALF_LANES, 1) * sin_signed


def _proj_kernel(x_ref, mod_ref, g1_ref, w_ref, qg_ref, kg_ref, hm_ref, cos_ref, sin_ref,
                 f_ref, q_ref, k_ref, v_ref, *, chunk):
    m = mod_ref[0]
    hm = hm_ref[...]
    col_f, col_q, col_k, col_v = (0, FOURIER_WIDTH, FOURIER_WIDTH + NA_WIDTH,
                                  FOURIER_WIDTH + 2 * NA_WIDTH)

    def project(c):
        rows = slice(c * chunk, (c + 1) * chunk)
        h = _rms_mod(x_ref[0, rows, :], g1_ref[...], m[0:1], m[1:2]).astype(BF16)
        return tuple(_dot(h, w_ref[:, col:col + NA_WIDTH]) for col in (col_q, col_k, col_f, col_v))

    def finish(c, pq, pk, pf, pv):
        rows = slice(c * chunk, (c + 1) * chunk)
        cos = cos_ref[rows, :]
        sin = sin_ref[rows, :]
        for p, gain_ref, dst in ((pq, qg_ref, q_ref), (pk, kg_ref, k_ref)):
            a = _head_norm(p, gain_ref[...], hm)
            blocks = [_rope(a[:, j:j + LANES], cos, sin) for j in range(0, NA_WIDTH, LANES)]
            dst[0, rows, :] = jnp.concatenate(blocks, axis=1).astype(BF16)
        f_ref[0, rows, :] = pf.astype(BF16)
        v_ref[0, rows, :] = pv.astype(BF16)

    n_chunks = x_ref.shape[1] // chunk
    pending = project(0)
    for c in range(n_chunks):
        upcoming = project(c + 1) if c + 1 < n_chunks else None
        finish(c, *pending)
        pending = upcoming


def _proj(x, mod, g1, w_in, qg, kg, head_mean, cos, sin, tm):
    b, s, d = x.shape
    out = jax.ShapeDtypeStruct((b, s, NA_WIDTH), BF16)
    tok = lambda t, i: (i, t, 0)
    const = lambda t, i: (0, 0)
    return pl.pallas_call(
        functools.partial(_proj_kernel, chunk=128),
        grid=(s // tm, b),
        in_specs=[pl.BlockSpec((1, tm, d), tok),
                  pl.BlockSpec((1, 8, d), lambda t, i: (i, 0, 0)),
                  pl.BlockSpec((1, d), const),
                  pl.BlockSpec(w_in.shape, const, pipeline_mode=pl.Buffered(1)),
                  pl.BlockSpec((1, NA_WIDTH), const),
                  pl.BlockSpec((1, NA_WIDTH), const),
                  pl.BlockSpec(head_mean.shape, const),
                  pl.BlockSpec((tm, cos.shape[1]), lambda t, i: (t, 0)),
                  pl.BlockSpec((tm, sin.shape[1]), lambda t, i: (t, 0))],
        out_specs=[pl.BlockSpec((1, tm, NA_WIDTH), tok)] * 4,
        out_shape=[out] * 4,
        compiler_params=_params("arbitrary", "arbitrary"),
        name="proj",
    )(x, mod, g1, w_in, qg, kg, head_mean, cos, sin)


def _pitch(n):
    tiles = -(-n // SUBLANES)
    return SUBLANES * (tiles + 1 - tiles % 2)


def _fourier_kernel(*refs, rows, b_chunk, n_cast):
    u_ref, cs_ref, w1_ref, g_ref = refs[:4]
    cast_src = refs[4:4 + n_cast]
    y_ref = refs[4 + n_cast]
    cast_dst = refs[5 + n_cast:5 + 2 * n_cast]
    ut_ref, z_ref, tr_ref, ti_ref = refs[5 + 2 * n_cast:]
    gd = FOURIER_GROUP_DIM
    w = GRID_W
    row_pitch = _pitch(rows)

    def reorder(a, carry):
        base = pl.multiple_of(a * w, w)
        ut_ref[pl.ds(a, w, stride=row_pitch), :] = u_ref[0, pl.ds(base, w), :].astype(F32)
        return carry

    def column(b):
        return ut_ref[pl.ds(pl.multiple_of(b * row_pitch, SUBLANES), rows), :].astype(BF16)

    def channel_stage(i, carry):
        b0 = i * b_chunk
        lhs = jnp.concatenate(
            [jnp.concatenate([column(b0 + 2 * p), column(b0 + 2 * p + 1)], axis=1)
             for p in range(b_chunk // 2)], axis=0)
        zc = _dot(lhs, cs_ref[...])
        for t in range(b_chunk):
            block = zc[(t // 2) * rows:(t // 2 + 1) * rows, (t % 2) * 2 * gd:(t % 2 + 1) * 2 * gd]
            z_ref[b0 + t, :rows] = block[:, :gd].astype(BF16)
            z_ref[b0 + t, rows:] = block[:, gd:].astype(BF16)
        return carry

    def row_stage(i, carry):
        t2 = _dot(w1_ref[...],
                  jnp.concatenate([z_ref[ROW_COLS * i + t] for t in range(ROW_COLS)], axis=1))
        for t in range(ROW_COLS):
            b = ROW_COLS * i + t
            for blk in range(rows // SUBLANES):
                dst = pl.multiple_of((blk * w + b) * SUBLANES, SUBLANES)
                lo = blk * SUBLANES
                tr_ref[pl.ds(dst, SUBLANES), :] = t2[lo:lo + SUBLANES, t * gd:(t + 1) * gd]
                ti_ref[pl.ds(dst, SUBLANES), :] = t2[rows + lo:rows + lo + SUBLANES,
                                                     t * gd:(t + 1) * gd]
        return carry

    def col_stage(blk, carry):
        for r_in in range(SUBLANES):
            ka = blk * SUBLANES + r_in
            start = blk * (w * SUBLANES) + r_in
            rhs = jnp.concatenate([tr_ref[pl.ds(start, w, stride=SUBLANES), :],
                                   ti_ref[pl.ds(start, w, stride=SUBLANES), :]], axis=0)
            y = _dot(g_ref[ka], rhs.astype(BF16))
            for kblk in range(w // SUBLANES):
                dst = pl.multiple_of((kblk * rows + ka) * SUBLANES, SUBLANES)
                ut_ref[pl.ds(dst, SUBLANES), :] = y[kblk * SUBLANES:(kblk + 1) * SUBLANES]
        return carry

    def emit(kblk, carry):
        for r_in in range(SUBLANES):
            kb = kblk * SUBLANES + r_in
            start = kblk * (rows * SUBLANES) + r_in
            dst = pl.multiple_of(kb * rows, rows)
            y_ref[0, pl.ds(dst, rows), :] = (
                ut_ref[pl.ds(start, rows, stride=SUBLANES), :].astype(BF16))
        return carry

    lax.fori_loop(0, rows, reorder, 0, unroll=8)
    for src, dst in zip(cast_src, cast_dst):
        dst[...] = src[...].astype(BF16)
    for i in range(w // b_chunk):
        channel_stage(i, 0)
    for i in range(w // ROW_COLS):
        row_stage(i, 0)
    for blk in range(rows // SUBLANES):
        col_stage(blk, 0)
    for kblk in range(w // SUBLANES):
        emit(kblk, 0)


def _fourier(f, cs, w1, g2, weights):
    b, s, width = f.shape
    rows = s // GRID_W
    gd = FOURIER_GROUP_DIM
    n_groups = width // gd
    n_steps = b * n_groups
    slab = pl.BlockSpec((1, s, gd), lambda i, g: (i, 0, g))
    const2 = lambda i, g: (0, 0)
    slices = [pl.BlockSpec((wt.shape[0] // n_steps, wt.shape[1]),
                           lambda i, g: (i * n_groups + g, 0)) for wt in weights]
    assert all(wt.shape[0] % (n_steps * 2 * SUBLANES) == 0 for wt in weights)
    out = pl.pallas_call(
        functools.partial(_fourier_kernel, rows=rows, b_chunk=8, n_cast=len(weights)),
        grid=(b, n_groups),
        in_specs=[slab,
                  pl.BlockSpec(cs.shape, const2),
                  pl.BlockSpec(w1.shape, const2),
                  pl.BlockSpec(g2.shape, lambda i, g: (0, 0, 0))] + slices,
        out_specs=[slab] + slices,
        out_shape=[jax.ShapeDtypeStruct(f.shape, BF16)]
        + [jax.ShapeDtypeStruct(wt.shape, BF16) for wt in weights],
        scratch_shapes=[pltpu.VMEM((GRID_W * _pitch(rows), gd), F32),
                        pltpu.VMEM((GRID_W, 2 * rows, gd), BF16),
                        pltpu.VMEM((rows * GRID_W, gd), F32),
                        pltpu.VMEM((rows * GRID_W, gd), F32)],
        compiler_params=_params("arbitrary", "arbitrary"),
        name="fourier",
    )(f, cs, w1, g2, *weights)
    return out[0], out[1:]


def _fold_lanes(op, *arrays):
    blocks = [a[:, j:j + LANES] for a in arrays for j in range(0, a.shape[1], LANES)]
    while len(blocks) > 1:
        blocks = [op(blocks[i], blocks[i + 1]) if i + 1 < len(blocks) else blocks[i]
                  for i in range(0, len(blocks), 2)]
    return blocks[0]


def _build_bias(rpb_ref, bias_ref):
    n_dr = 2 * WIN_ROWS - 1
    col = lax.broadcasted_iota(jnp.int32, (GRID_W, LANES), 0)
    lane = lax.broadcasted_iota(jnp.int32, (GRID_W, LANES), 1)
    kcol = lane % GRID_W
    col_start = jnp.clip(col - WIN_COLS // 2, 0, GRID_W - WIN_COLS)
    valid = (kcol >= col_start) & (kcol < col_start + WIN_COLS)
    low = lane < GRID_W

    def per_head(h, carry):
        pair = h // 2
        row0 = pl.multiple_of((h % 2) * GRID_W, GRID_W)
        at_low, at_high = [], []
        for dr in range(n_dr):
            p = jnp.broadcast_to(rpb_ref[pl.ds(h * n_dr + dr, 1), :], (GRID_W, LANES))
            at_low.append(pltpu.roll(p, LANES - GRID_W + 1, 1, stride=1, stride_axis=0))
            at_high.append(pltpu.roll(p, LANES - 2 * GRID_W + 1, 1, stride=1, stride_axis=0))
        for variant in range(WIN_ROWS):
            for jp in range(WIN_ROWS // 2):
                dr = 2 * jp - variant + WIN_ROWS - 1
                block = jnp.where(valid, jnp.where(low, at_low[dr], at_high[dr + 1]), MASK_VALUE)
                bias_ref[variant, pair, pl.ds(row0, GRID_W), jp * LANES:(jp + 1) * LANES] = block
        return carry

    lax.fori_loop(0, NA_HEADS, per_head, 0)


def _kv_block_start(step, rows_per_step, n_rows):
    span = rows_per_step + WIN_ROWS - 1
    return jnp.clip(step * rows_per_step - WIN_ROWS // 2, 0, n_rows - span)


def _attn_kernel(q_ref, k_ref, v_ref, kc_ref, vc_ref, rpb_ref, o_ref, bias_ref, *,
                 rows_per_step, n_rows):
    @pl.when((pl.program_id(0) == 0) & (pl.program_id(1) == 0))
    def _():
        _build_bias(rpb_ref, bias_ref)

    n_loc = WIN_ROWS * GRID_W
    lane = lax.broadcasted_iota(jnp.int32, (GRID_W, LANES), 1)
    low = lane < HEAD_DIM
    first = (lane % HALF_LANES) < HEAD_LANES
    units = [(rr, pair) for rr in range(rows_per_step) for pair in range(NA_WIDTH // LANES)]

    def window(rr):
        r = pl.program_id(1) * rows_per_step + rr
        row_start = jnp.clip(r - WIN_ROWS // 2, 0, n_rows - WIN_ROWS)
        block_start = _kv_block_start(pl.program_id(1), rows_per_step, n_rows)
        return r - row_start, pl.multiple_of((row_start - block_start) * GRID_W, GRID_W)

    def scores(rr, pair):
        variant, start = window(rr)
        sl = slice(pair * LANES, (pair + 1) * LANES)
        qp = q_ref[0, rr * GRID_W:(rr + 1) * GRID_W, sl]
        zero = jnp.zeros_like(qp)
        q2 = jnp.concatenate([jnp.where(first, qp, zero), jnp.where(first, zero, qp)], axis=0)
        s_loc = _dot_nt(q2, k_ref[0, pl.ds(start, n_loc), sl]) + bias_ref[variant, pair]
        s_ctx = _dot_nt(q2, kc_ref[0, :, sl])
        return s_loc, s_ctx

    def finish(rr, pair, s_loc, s_ctx):
        _, start = window(rr)
        sl = slice(pair * LANES, (pair + 1) * LANES)
        m = jnp.max(_fold_lanes(jnp.maximum, s_loc, s_ctx), axis=-1, keepdims=True)
        p_loc = jnp.exp2(s_loc - m)
        p_ctx = jnp.exp2(s_ctx - m)
        denom = jnp.sum(_fold_lanes(jnp.add, p_loc, p_ctx), axis=-1, keepdims=True)
        o = (_dot(p_loc.astype(BF16), v_ref[0, pl.ds(start, n_loc), sl])
             + _dot(p_ctx.astype(BF16), vc_ref[0, :, sl]))
        o = o / denom
        o_ref[0, rr * GRID_W:(rr + 1) * GRID_W, sl] = (
            jnp.where(low, o[:GRID_W], o[GRID_W:]).astype(BF16))

    pending = [scores(*u) for u in units[:SCORE_LOOKAHEAD]]
    for i, unit in enumerate(units):
        if i + SCORE_LOOKAHEAD < len(units):
            pending.append(scores(*units[i + SCORE_LOOKAHEAD]))
        finish(*unit, *pending.pop(0))


def _attn(q, k, v, kc, vc, rpb_rows, rows_per_step):
    b, s, width = q.shape
    n_rows = s // GRID_W
    c = kc.shape[1]
    tq = rows_per_step * GRID_W
    whole = lambda i, r: (i, 0, 0)
    heads_per_block = LANES // HEAD_DIM
    bias_shape = (WIN_ROWS, NA_HEADS // heads_per_block, heads_per_block * GRID_W,
                  WIN_ROWS * GRID_W)
    kv_rows = rows_per_step + WIN_ROWS - 1
    kv_block = pl.BlockSpec(
        (pl.Element(1), pl.Element(kv_rows * GRID_W), pl.Element(width)),
        lambda i, r: (i, _kv_block_start(r, rows_per_step, n_rows) * GRID_W, 0))
    return pl.pallas_call(
        functools.partial(_attn_kernel, rows_per_step=rows_per_step, n_rows=n_rows),
        grid=(b, n_rows // rows_per_step),
        in_specs=[pl.BlockSpec((1, tq, width), lambda i, r: (i, r, 0)),
                  kv_block,
                  kv_block,
                  pl.BlockSpec((1, c, width), whole),
                  pl.BlockSpec((1, c, width), whole),
                  pl.BlockSpec(rpb_rows.shape, lambda i, r: (0, 0))],
        out_specs=pl.BlockSpec((1, tq, width), lambda i, r: (i, r, 0)),
        out_shape=jax.ShapeDtypeStruct((b, s, width), BF16),
        scratch_shapes=[pltpu.VMEM(bias_shape, F32)],
        compiler_params=_params("arbitrary", "arbitrary"),
        name="attn",
    )(q, k, v, kc, vc, rpb_rows)


def _merge_kernel(x_ref, ya_ref, yf_ref, mod_ref, g1_ref, wg_ref, wfo_ref, wao_ref, wo_ref,
                  o_ref, *, chunk):
    d = g1_ref.shape[1]
    m = mod_ref[0]

    def branches(c):
        rows = slice(c * chunk, (c + 1) * chunk)
        h = _rms_mod(x_ref[0, rows, :], g1_ref[...], m[0:1], m[1:2]).astype(BF16)
        return (_dot(h, wg_ref[...]), _dot(yf_ref[0, rows, :], wfo_ref[...]),
                _dot(ya_ref[0, rows, :], wao_ref[...]))

    def finish(c, gate_logits, four, attn):
        rows = slice(c * chunk, (c + 1) * chunk)
        gates = jax.nn.sigmoid(gate_logits)
        merged = gates[:, :d] * four + gates[:, d:] * attn
        o_ref[0, rows, :] = x_ref[0, rows, :] + m[2:3] * _dot(merged.astype(BF16), wo_ref[...])

    n_chunks = x_ref.shape[1] // chunk
    pending = branches(0)
    for c in range(n_chunks):
        upcoming = branches(c + 1) if c + 1 < n_chunks else None
        finish(c, *pending)
        pending = upcoming


def _merge(x, y_attn, y_four, mod, g1, w_gate, w_fo, w_ao, w_o, tm):
    b, s, d = x.shape
    const = lambda i, t: (0, 0)
    tok = lambda i, t: (i, t, 0)
    return pl.pallas_call(
        functools.partial(_merge_kernel, chunk=256),
        grid=(b, s // tm),
        in_specs=[pl.BlockSpec((1, tm, d), tok),
                  pl.BlockSpec((1, tm, NA_WIDTH), tok),
                  pl.BlockSpec((1, tm, FOURIER_WIDTH), tok),
                  pl.BlockSpec((1, 8, d), lambda i, t: (i, 0, 0)),
                  pl.BlockSpec((1, d), const),
                  pl.BlockSpec(w_gate.shape, const, pipeline_mode=pl.Buffered(1)),
                  pl.BlockSpec(w_fo.shape, const, pipeline_mode=pl.Buffered(1)),
                  pl.BlockSpec(w_ao.shape, const, pipeline_mode=pl.Buffered(1)),
                  pl.BlockSpec(w_o.shape, const, pipeline_mode=pl.Buffered(1))],
        out_specs=pl.BlockSpec((1, tm, d), tok),
        out_shape=jax.ShapeDtypeStruct(x.shape, F32),
        compiler_params=_params("arbitrary", "arbitrary"),
        name="merge",
    )(x, y_attn, y_four, mod, g1, w_gate, w_fo, w_ao, w_o)


def _mlp_kernel(x_ref, mod_ref, g2_ref, w1_ref, w2_ref, o_ref, *, ff_chunk):
    x = x_ref[0]
    m = mod_ref[0]
    h = _rms_mod(x, g2_ref[...], m[3:4], m[4:5]).astype(BF16)
    acc = jnp.zeros(x.shape, F32)
    for c in range(w1_ref.shape[1] // ff_chunk):
        a = jnp.maximum(_dot(h, w1_ref[:, c * ff_chunk:(c + 1) * ff_chunk]), 0.0)
        acc = acc + _dot((a * a).astype(BF16), w2_ref[c * ff_chunk:(c + 1) * ff_chunk, :])
    o_ref[0] = x + m[5:6] * acc


def _mlp(x, mod, g2, w1, w2, tm, ff_chunk):
    b, s, d = x.shape
    const = lambda i, t: (0, 0)
    return pl.pallas_call(
        functools.partial(_mlp_kernel, ff_chunk=ff_chunk),
        grid=(b, s // tm),
        in_specs=[pl.BlockSpec((1, tm, d), lambda i, t: (i, t, 0)),
                  pl.BlockSpec((1, 8, d), lambda i, t: (i, 0, 0)),
                  pl.BlockSpec((1, d), const),
                  pl.BlockSpec(w1.shape, const, pipeline_mode=pl.Buffered(1)),
                  pl.BlockSpec(w2.shape, const, pipeline_mode=pl.Buffered(1))],
        out_specs=pl.BlockSpec((1, tm, d), lambda i, t: (i, t, 0)),
        out_shape=jax.ShapeDtypeStruct(x.shape, F32),
        compiler_params=_params("arbitrary", "arbitrary"),
        name="mlp",
    )(x, mod, g2, w1, w2)


def _dft_tables(rows):
    n = rows * GRID_W
    gd = FOURIER_GROUP_DIM
    c = np.arange(gd)
    ang = 2.0 * np.pi * np.outer(c, c) / gd
    cs = np.concatenate([np.cos(ang), np.sin(ang)], axis=1) * 2.0 ** -3
    cs = np.kron(np.eye(2), cs)
    a = np.arange(rows)
    ang = 2.0 * np.pi * np.outer(a, a) / rows
    c1, s1 = np.cos(ang), np.sin(ang)
    w1 = np.block([[c1, -s1], [s1, c1]]) * 2.0 ** -4
    b = np.arange(GRID_W)
    k = a[:, None] + rows * b[None, :]
    ang = 2.0 * np.pi * k[:, :, None] * b[None, None, :] / n
    g2 = np.concatenate([np.cos(ang), -np.sin(ang)], axis=2) * 2.0 ** -3
    assert gd * rows * GRID_W == 2 ** 20
    return tuple(jnp.asarray(m, F32).astype(BF16) for m in (cs, w1, g2))


def _qk_lane_perm():
    lane = np.arange(NA_WIDTH)
    pair, in_block = lane // LANES, lane % LANES
    side, in_side = in_block // HALF_LANES, in_block % HALF_LANES
    head = pair * (LANES // HEAD_DIM) + in_side // HEAD_LANES
    axis, freq = (in_side % HEAD_LANES) // N_FREQ, in_side % N_FREQ
    return head * HEAD_DIM + axis * (2 * N_FREQ) + side * N_FREQ + freq


def _gain_lanes(g):
    sides = g.reshape(2, 2, N_FREQ).transpose(1, 0, 2).reshape(2, HEAD_LANES)
    block = jnp.concatenate([jnp.tile(sides[0], LANES // HEAD_DIM),
                             jnp.tile(sides[1], LANES // HEAD_DIM)])
    return jnp.tile(block, NA_WIDTH // LANES)[None, :]


def _rope_tables(seq):
    t = np.arange(seq)
    axis_dim = 2 * N_FREQ
    inv_freq = ROPE_THETA ** (-np.arange(0, axis_dim, 2, dtype=np.float64) / axis_dim)
    ang_r = (t // GRID_W)[:, None] * inv_freq
    ang_c = (t % GRID_W)[:, None] * inv_freq
    ang = np.tile(np.concatenate([ang_r, ang_c], axis=1), (1, LANES // HEAD_LANES))
    sign = np.repeat([-1.0, 1.0], HALF_LANES)
    return jnp.asarray(np.cos(ang), F32), jnp.asarray(np.sin(ang) * sign, F32)


def _rpb_rows(rpb):
    left = GRID_W - WIN_COLS
    padded = jnp.pad(rpb * LOG2_E, ((0, 0), (0, 0), (left, LANES - left - rpb.shape[-1])))
    return padded.reshape(rpb.shape[0] * rpb.shape[1], LANES)


def kernel(x, c, ctx, c_ctx, w_ada, b_ada, norm1_g, norm2_g, w_in, q_norm_g, k_norm_g, rpb,
           w_branch_gate, w_fourier_out, w_attn_out, w_out, w_mlp1, w_mlp2):
    b, s, d = x.shape
    assert w_ada.shape[0] == 1, "single-layer block only"
    n_rows = s // GRID_W
    assert n_rows >= WIN_ROWS and b < 8

    cond = jnp.zeros((8, d), F32).at[:b].set(c).at[b].set(c_ctx)
    mods = _ada(cond, w_ada[0], b_ada)
    mod = jnp.pad(mods[:b].reshape(b, N_MOD, d), ((0, 0), (0, 8 - N_MOD), (0, 0)))
    mod_ctx = jnp.pad(mods[b].reshape(N_MOD, d), ((0, 8 - N_MOD), (0, 0)))

    perm = _qk_lane_perm()
    head_id = perm[:NA_WIDTH // 2] // HEAD_DIM
    head_mean = jnp.asarray((head_id[:, None] == head_id[None, :]) / HEAD_DIM, F32).astype(BF16)
    perm_matrix = jnp.asarray(np.arange(NA_WIDTH)[:, None] == perm[None, :], F32).astype(BF16)
    qg = _gain_lanes(q_norm_g[0]) * (HEAD_DIM ** -0.5 * LOG2_E)
    kg = _gain_lanes(k_norm_g[0])
    cos, sin = _rope_tables(s)
    cs, w1, g2 = _dft_tables(n_rows)
    w_in_b = _w_in_bf16(w_in[0], perm_matrix, tr=256)

    kc, vc = _ctx_kv(ctx, mod_ctx, norm1_g, w_in_b, kg, head_mean)
    f, q, k, v = _proj(x, mod, norm1_g, w_in_b, qg, kg, head_mean, cos, sin, tm=2048)
    y_four, (w_gate_b, w_fo_b, w_ao_b, w_o_b, w_mlp1_b, w_mlp2_b) = _fourier(
        f, cs, w1, g2, [w_branch_gate[0], w_fourier_out[0], w_attn_out[0], w_out[0],
                        w_mlp1[0], w_mlp2[0]])
    y_attn = _attn(q, k, v, kc, vc, _rpb_rows(rpb[0]), rows_per_step=16)
    x1 = _merge(x, y_attn, y_four, mod, norm1_g, w_gate_b, w_fo_b, w_ao_b, w_o_b, tm=1024)
    return _mlp(x1, mod, norm2_g, w_mlp1_b, w_mlp2_b, tm=1024, ff_chunk=1024)
```

```python
import functools

import numpy as np
import jax
import jax.numpy as jnp
from jax import lax
from jax.experimental import pallas as pl
from jax.experimental.pallas import tpu as pltpu

GRID_W = 64
N_FOURIER_GROUPS = 4
FOURIER_GROUP_DIM = 128
FOURIER_WIDTH = N_FOURIER_GROUPS * FOURIER_GROUP_DIM
NA_HEADS = 8
HEAD_DIM = 64
NA_WIDTH = NA_HEADS * HEAD_DIM
WIN_ROWS = 8
WIN_COLS = 16
ROPE_THETA = 10000.0
NORM_EPS = 1e-6
N_MOD = 6
MASK_VALUE = -1e30
LOG2_E = 1.4426950408889634
SCORE_LOOKAHEAD = 2
ROW_COLS = 4

N_FREQ = HEAD_DIM // 4

LANES = 128
SUBLANES = 8
VMEM_LIMIT_BYTES = 48 * 1024 * 1024

HALF_LANES = LANES // 2
HEAD_LANES = HALF_LANES // (LANES // HEAD_DIM)

BF16 = jnp.bfloat16
F32 = jnp.float32


def _dot(a, b):
    return jnp.dot(a, b, preferred_element_type=F32)


def _dot_nt(a, b):
    return lax.dot_general(a, b, (((1,), (1,)), ((), ())), preferred_element_type=F32)


def _params(*semantics):
    return pltpu.CompilerParams(dimension_semantics=semantics,
                                vmem_limit_bytes=VMEM_LIMIT_BYTES)


def _rms_mod(x, g, shift, scale):
    ms = jnp.mean(x * x, axis=-1, keepdims=True)
    return (x * lax.rsqrt(ms + NORM_EPS) * g) * (1.0 + scale) + shift


def _head_norm(a, gain, head_mean):
    half = NA_WIDTH // 2
    sq = (a * a).astype(BF16)
    ms = jnp.concatenate([_dot(sq[:, :half], head_mean), _dot(sq[:, half:], head_mean)], axis=1)
    return a * lax.rsqrt(ms + NORM_EPS) * gain


def _ada_kernel(cond_ref, w_ref, b_ref, o_ref):
    cond = cond_ref[...]
    s = cond * jax.nn.sigmoid(cond)
    s_hi = s.astype(BF16)
    s_lo = (s - s_hi.astype(F32)).astype(BF16)
    w = w_ref[...]
    w_hi = w.astype(BF16)
    w_lo = (w - w_hi.astype(F32)).astype(BF16)
    o_ref[...] = _dot(s_hi, w_hi) + _dot(s_hi, w_lo) + _dot(s_lo, w_hi) + b_ref[...]


def _w_in_kernel(w_ref, p_ref, o_ref):
    wb = w_ref[...].astype(BF16)
    v_col = FOURIER_WIDTH + 2 * NA_WIDTH
    o_ref[:, :FOURIER_WIDTH] = wb[:, :FOURIER_WIDTH]
    o_ref[:, v_col:] = wb[:, v_col:]
    for col in (FOURIER_WIDTH, FOURIER_WIDTH + NA_WIDTH):
        o_ref[:, col:col + NA_WIDTH] = _dot(wb[:, col:col + NA_WIDTH], p_ref[...]).astype(BF16)


def _prep_kernel(cond_ref, wa_ref, b_ref, w_ref, p_ref, mods_ref, o_ref):
    _ada_kernel(cond_ref, wa_ref, b_ref, mods_ref)
    _w_in_kernel(w_ref, p_ref, o_ref)


def _prep(cond, w_ada, b_ada, w_in, perm_matrix, steps):
    rows, d = cond.shape
    n_mod = w_ada.shape[1]
    n_in = w_in.shape[1]
    tn, tr = n_mod // steps, d // steps
    return pl.pallas_call(
        _prep_kernel,
        grid=(steps,),
        in_specs=[pl.BlockSpec((rows, d), lambda j: (0, 0)),
                  pl.BlockSpec((d, tn), lambda j: (0, j)),
                  pl.BlockSpec((1, tn), lambda j: (0, j)),
                  pl.BlockSpec((tr, n_in), lambda j: (j, 0)),
                  pl.BlockSpec(perm_matrix.shape, lambda j: (0, 0))],
        out_specs=[pl.BlockSpec((rows, tn), lambda j: (0, j)),
                   pl.BlockSpec((tr, n_in), lambda j: (j, 0))],
        out_shape=[jax.ShapeDtypeStruct((rows, n_mod), F32),
                   jax.ShapeDtypeStruct((d, n_in), BF16)],
        compiler_params=_params("arbitrary"),
        name="prep",
    )(cond, w_ada, b_ada, w_in, perm_matrix)


def _ctx_kv_kernel(ctx_ref, mod_ref, g1_ref, w_ref, kg_ref, hm_ref, k_ref, v_ref):
    m = mod_ref[...]
    h = _rms_mod(ctx_ref[0], g1_ref[...], m[0:1], m[1:2])
    p = _dot(h.astype(BF16), w_ref[...])
    k_ref[0] = _head_norm(p[:, :NA_WIDTH], kg_ref[...], hm_ref[...]).astype(BF16)
    v_ref[0] = p[:, NA_WIDTH:].astype(BF16)


def _ctx_kv(ctx, mod_ctx, g1, w_in, kg, head_mean):
    b, c, d = ctx.shape
    kv_cols = 2 * NA_WIDTH
    out = jax.ShapeDtypeStruct((b, c, NA_WIDTH), BF16)
    return pl.pallas_call(
        _ctx_kv_kernel,
        grid=(b,),
        in_specs=[pl.BlockSpec((1, c, d), lambda i: (i, 0, 0)),
                  pl.BlockSpec((8, d), lambda i: (0, 0)),
                  pl.BlockSpec((1, d), lambda i: (0, 0)),
                  pl.BlockSpec((d, kv_cols), lambda i: (0, (w_in.shape[1] - kv_cols) // kv_cols)),
                  pl.BlockSpec((1, NA_WIDTH), lambda i: (0, 0)),
                  pl.BlockSpec(head_mean.shape, lambda i: (0, 0))],
        out_specs=[pl.BlockSpec((1, c, NA_WIDTH), lambda i: (i, 0, 0))] * 2,
        out_shape=[out, out],
        compiler_params=_params("arbitrary"),
        name="ctx_kv",
    )(ctx, mod_ctx, g1, w_in, kg, head_mean)


def _rope(a, cos, sin_signed):
    return a * cos + pltpu.roll(a, HALF_LANES, 1) * sin_signed


def _proj_kernel(x_ref, mod_ref, g1_ref, w_ref, qg_ref, kg_ref, hm_ref, cos_ref, sin_ref,
                 f_ref, q_ref, k_ref, v_ref, *, chunk):
    m = mod_ref[0]
    hm = hm_ref[...]
    col_f, col_q, col_k, col_v = (0, FOURIER_WIDTH, FOURIER_WIDTH + NA_WIDTH,
                                  FOURIER_WIDTH + 2 * NA_WIDTH)

    def project(c):
        rows = slice(c * chunk, (c + 1) * chunk)
        h = _rms_mod(x_ref[0, rows, :], g1_ref[...], m[0:1], m[1:2]).astype(BF16)
        return tuple(_dot(h, w_ref[:, col:col + NA_WIDTH]) for col in (col_q, col_k, col_f, col_v))

    def finish(c, pq, pk, pf, pv):
        rows = slice(c * chunk, (c + 1) * chunk)
        cos = cos_ref[rows, :]
        sin = sin_ref[rows, :]
        for p, gain_ref, dst in ((pq, qg_ref, q_ref), (pk, kg_ref, k_ref)):
            a = _head_norm(p, gain_ref[...], hm)
            blocks = [_rope(a[:, j:j + LANES], cos, sin) for j in range(0, NA_WIDTH, LANES)]
            dst[0, rows, :] = jnp.concatenate(blocks, axis=1).astype(BF16)
        f_ref[0, rows, :] = pf.astype(BF16)
        v_ref[0, rows, :] = pv.astype(BF16)

    n_chunks = x_ref.shape[1] // chunk
    pending = project(0)
    for c in range(n_chunks):
        upcoming = project(c + 1) if c + 1 < n_chunks else None
        finish(c, *pending)
        pending = upcoming


def _proj(x, mod, g1, w_in, qg, kg, head_mean, cos, sin, tm):
    b, s, d = x.shape
    out = jax.ShapeDtypeStruct((b, s, NA_WIDTH), BF16)
    tok = lambda t, i: (i, t, 0)
    const = lambda t, i: (0, 0)
    return pl.pallas_call(
        functools.partial(_proj_kernel, chunk=128),
        grid=(s // tm, b),
        in_specs=[pl.BlockSpec((1, tm, d), tok),
                  pl.BlockSpec((1, 8, d), lambda t, i: (i, 0, 0)),
                  pl.BlockSpec((1, d), const),
                  pl.BlockSpec(w_in.shape, const, pipeline_mode=pl.Buffered(1)),
                  pl.BlockSpec((1, NA_WIDTH), const),
                  pl.BlockSpec((1, NA_WIDTH), const),
                  pl.BlockSpec(head_mean.shape, const),
                  pl.BlockSpec((tm, cos.shape[1]), lambda t, i: (t, 0)),
                  pl.BlockSpec((tm, sin.shape[1]), lambda t, i: (t, 0))],
        out_specs=[pl.BlockSpec((1, tm, NA_WIDTH), tok)] * 4,
        out_shape=[out] * 4,
        compiler_params=_params("arbitrary", "arbitrary"),
        name="proj",
    )(x, mod, g1, w_in, qg, kg, head_mean, cos, sin)


def _pitch(n):
    tiles = -(-n // SUBLANES)
    return SUBLANES * (tiles + 1 - tiles % 2)


def _fourier_kernel(*refs, rows, b_chunk, n_cast):
    u_ref, cs_ref, w1_ref, g_ref = refs[:4]
    cast_src = refs[4:4 + n_cast]
    y_ref = refs[4 + n_cast]
    cast_dst = refs[5 + n_cast:5 + 2 * n_cast]
    ut_ref, z_ref, tr_ref, ti_ref = refs[5 + 2 * n_cast:]
    gd = FOURIER_GROUP_DIM
    w = GRID_W
    row_pitch = _pitch(rows)

    def reorder(a, carry):
        base = pl.multiple_of(a * w, w)
        ut_ref[pl.ds(a, w, stride=row_pitch), :] = u_ref[0, pl.ds(base, w), :].astype(F32)
        return carry

    def column(b):
        return ut_ref[pl.ds(pl.multiple_of(b * row_pitch, SUBLANES), rows), :].astype(BF16)

    def channel_stage(i, carry):
        b0 = i * b_chunk
        lhs = jnp.concatenate(
            [jnp.concatenate([column(b0 + 2 * p), column(b0 + 2 * p + 1)], axis=1)
             for p in range(b_chunk // 2)], axis=0)
        zc = _dot(lhs, cs_ref[...])
        for t in range(b_chunk):
            block = zc[(t // 2) * rows:(t // 2 + 1) * rows, (t % 2) * 2 * gd:(t % 2 + 1) * 2 * gd]
            z_ref[b0 + t, :rows] = block[:, :gd].astype(BF16)
            z_ref[b0 + t, rows:] = block[:, gd:].astype(BF16)
        return carry

    def row_stage(i, carry):
        t2 = _dot(w1_ref[...],
                  jnp.concatenate([z_ref[ROW_COLS * i + t] for t in range(ROW_COLS)], axis=1))
        for t in range(ROW_COLS):
            b = ROW_COLS * i + t
            for blk in range(rows // SUBLANES):
                dst = pl.multiple_of((blk * w + b) * SUBLANES, SUBLANES)
                lo = blk * SUBLANES
                tr_ref[pl.ds(dst, SUBLANES), :] = t2[lo:lo + SUBLANES, t * gd:(t + 1) * gd]
                ti_ref[pl.ds(dst, SUBLANES), :] = t2[rows + lo:rows + lo + SUBLANES,
                                                     t * gd:(t + 1) * gd]
        return carry

    def col_stage(blk, carry):
        for r_in in range(SUBLANES):
            ka = blk * SUBLANES + r_in
            start = blk * (w * SUBLANES) + r_in
            rhs = jnp.concatenate([tr_ref[pl.ds(start, w, stride=SUBLANES), :],
                                   ti_ref[pl.ds(start, w, stride=SUBLANES), :]], axis=0)
            y = _dot(g_ref[ka], rhs.astype(BF16))
            for kblk in range(w // SUBLANES):
                dst = pl.multiple_of((kblk * rows + ka) * SUBLANES, SUBLANES)
                ut_ref[pl.ds(dst, SUBLANES), :] = y[kblk * SUBLANES:(kblk + 1) * SUBLANES]
        return carry

    def emit(kblk, carry):
        for r_in in range(SUBLANES):
            kb = kblk * SUBLANES + r_in
            start = kblk * (rows * SUBLANES) + r_in
            dst = pl.multiple_of(kb * rows, rows)
            y_ref[0, pl.ds(dst, rows), :] = (
                ut_ref[pl.ds(start, rows, stride=SUBLANES), :].astype(BF16))
        return carry

    lax.fori_loop(0, rows, reorder, 0, unroll=8)
    for src, dst in zip(cast_src, cast_dst):
        dst[...] = src[...].astype(BF16)
    for i in range(w // b_chunk):
        channel_stage(i, 0)
    for i in range(w // ROW_COLS):
        row_stage(i, 0)
    for blk in range(rows // SUBLANES):
        col_stage(blk, 0)
    for kblk in range(w // SUBLANES):
        emit(kblk, 0)


def _fourier(f, cs, w1, g2, weights):
    b, s, width = f.shape
    rows = s // GRID_W
    gd = FOURIER_GROUP_DIM
    n_groups = width // gd
    n_steps = b * n_groups
    slab = pl.BlockSpec((1, s, gd), lambda i, g: (i, 0, g))
    const2 = lambda i, g: (0, 0)
    slices = [pl.BlockSpec((wt.shape[0] // n_steps, wt.shape[1]),
                           lambda i, g: (i * n_groups + g, 0)) for wt in weights]
    assert all(wt.shape[0] % (n_steps * 2 * SUBLANES) == 0 for wt in weights)
    out = pl.pallas_call(
        functools.partial(_fourier_kernel, rows=rows, b_chunk=8, n_cast=len(weights)),
        grid=(b, n_groups),
        in_specs=[slab,
                  pl.BlockSpec(cs.shape, const2),
                  pl.BlockSpec(w1.shape, const2),
                  pl.BlockSpec(g2.shape, lambda i, g: (0, 0, 0))] + slices,
        out_specs=[slab] + slices,
        out_shape=[jax.ShapeDtypeStruct(f.shape, BF16)]
        + [jax.ShapeDtypeStruct(wt.shape, BF16) for wt in weights],
        scratch_shapes=[pltpu.VMEM((GRID_W * _pitch(rows), gd), F32),
                        pltpu.VMEM((GRID_W, 2 * rows, gd), BF16),
                        pltpu.VMEM((rows * GRID_W, gd), F32),
                        pltpu.VMEM((rows * GRID_W, gd), F32)],
        compiler_params=_params("arbitrary", "arbitrary"),
        name="fourier",
    )(f, cs, w1, g2, *weights)
    return out[0], out[1:]


def _fold_lanes(op, *arrays):
    blocks = [a[:, j:j + LANES] for a in arrays for j in range(0, a.shape[1], LANES)]
    while len(blocks) > 1:
        blocks = [op(blocks[i], blocks[i + 1]) if i + 1 < len(blocks) else blocks[i]
                  for i in range(0, len(blocks), 2)]
    return blocks[0]


def _build_bias(rpb_ref, bias_ref):
    n_dr = 2 * WIN_ROWS - 1
    col = lax.broadcasted_iota(jnp.int32, (GRID_W, LANES), 0)
    lane = lax.broadcasted_iota(jnp.int32, (GRID_W, LANES), 1)
    kcol = lane % GRID_W
    col_start = jnp.clip(col - WIN_COLS // 2, 0, GRID_W - WIN_COLS)
    valid = (kcol >= col_start) & (kcol < col_start + WIN_COLS)
    low = lane < GRID_W

    def per_head(h, carry):
        pair = h // 2
        row0 = pl.multiple_of((h % 2) * GRID_W, GRID_W)
        at_low, at_high = [], []
        for dr in range(n_dr):
            p = jnp.broadcast_to(rpb_ref[pl.ds(h * n_dr + dr, 1), :], (GRID_W, LANES))
            at_low.append(pltpu.roll(p, LANES - GRID_W + 1, 1, stride=1, stride_axis=0))
            at_high.append(pltpu.roll(p, LANES - 2 * GRID_W + 1, 1, stride=1, stride_axis=0))
        for variant in range(WIN_ROWS):
            for jp in range(WIN_ROWS // 2):
                dr = 2 * jp - variant + WIN_ROWS - 1
                block = jnp.where(valid, jnp.where(low, at_low[dr], at_high[dr + 1]), MASK_VALUE)
                bias_ref[variant, pair, pl.ds(row0, GRID_W), jp * LANES:(jp + 1) * LANES] = block
        return carry

    lax.fori_loop(0, NA_HEADS, per_head, 0)


def _kv_block_start(step, rows_per_step, n_rows):
    span = rows_per_step + WIN_ROWS - 1
    return jnp.clip(step * rows_per_step - WIN_ROWS // 2, 0, n_rows - span)


def _attn_kernel(q_ref, k_ref, v_ref, kc_ref, vc_ref, rpb_ref, o_ref, bias_ref, *,
                 rows_per_step, n_rows):
    @pl.when((pl.program_id(0) == 0) & (pl.program_id(1) == 0))
    def _():
        _build_bias(rpb_ref, bias_ref)

    n_loc = WIN_ROWS * GRID_W
    lane = lax.broadcasted_iota(jnp.int32, (GRID_W, LANES), 1)
    low = lane < HEAD_DIM
    first = (lane % HALF_LANES) < HEAD_LANES
    units = [(rr, pair) for rr in range(rows_per_step) for pair in range(NA_WIDTH // LANES)]

    def window(rr):
        r = pl.program_id(1) * rows_per_step + rr
        row_start = jnp.clip(r - WIN_ROWS // 2, 0, n_rows - WIN_ROWS)
        block_start = _kv_block_start(pl.program_id(1), rows_per_step, n_rows)
        return r - row_start, pl.multiple_of((row_start - block_start) * GRID_W, GRID_W)

    def scores(rr, pair):
        variant, start = window(rr)
        sl = slice(pair * LANES, (pair + 1) * LANES)
        qp = q_ref[0, rr * GRID_W:(rr + 1) * GRID_W, sl]
        zero = jnp.zeros_like(qp)
        q2 = jnp.concatenate([jnp.where(first, qp, zero), jnp.where(first, zero, qp)], axis=0)
        s_loc = _dot_nt(q2, k_ref[0, pl.ds(start, n_loc), sl]) + bias_ref[variant, pair]
        s_ctx = _dot_nt(q2, kc_ref[0, :, sl])
        return s_loc, s_ctx

    def finish(rr, pair, s_loc, s_ctx):
        _, start = window(rr)
        sl = slice(pair * LANES, (pair + 1) * LANES)
        m = jnp.max(_fold_lanes(jnp.maximum, s_loc, s_ctx), axis=-1, keepdims=True)
        p_loc = jnp.exp2(s_loc - m)
        p_ctx = jnp.exp2(s_ctx - m)
        denom = jnp.sum(_fold_lanes(jnp.add, p_loc, p_ctx), axis=-1, keepdims=True)
        o = (_dot(p_loc.astype(BF16), v_ref[0, pl.ds(start, n_loc), sl])
             + _dot(p_ctx.astype(BF16), vc_ref[0, :, sl]))
        o = o / denom
        o_ref[0, rr * GRID_W:(rr + 1) * GRID_W, sl] = (
            jnp.where(low, o[:GRID_W], o[GRID_W:]).astype(BF16))

    pending = [scores(*u) for u in units[:SCORE_LOOKAHEAD]]
    for i, unit in enumerate(units):
        if i + SCORE_LOOKAHEAD < len(units):
            pending.append(scores(*units[i + SCORE_LOOKAHEAD]))
        finish(*unit, *pending.pop(0))


def _attn(q, k, v, kc, vc, rpb_rows, rows_per_step):
    b, s, width = q.shape
    n_rows = s // GRID_W
    c = kc.shape[1]
    tq = rows_per_step * GRID_W
    whole = lambda i, r: (i, 0, 0)
    heads_per_block = LANES // HEAD_DIM
    bias_shape = (WIN_ROWS, NA_HEADS // heads_per_block, heads_per_block * GRID_W,
                  WIN_ROWS * GRID_W)
    kv_rows = rows_per_step + WIN_ROWS - 1
    kv_block = pl.BlockSpec(
        (pl.Element(1), pl.Element(kv_rows * GRID_W), pl.Element(width)),
        lambda i, r: (i, _kv_block_start(r, rows_per_step, n_rows) * GRID_W, 0))
    return pl.pallas_call(
        functools.partial(_attn_kernel, rows_per_step=rows_per_step, n_rows=n_rows),
        grid=(b, n_rows // rows_per_step),
        in_specs=[pl.BlockSpec((1, tq, width), lambda i, r: (i, r, 0)),
                  kv_block,
                  kv_block,
                  pl.BlockSpec((1, c, width), whole),
                  pl.BlockSpec((1, c, width), whole),
                  pl.BlockSpec(rpb_rows.shape, lambda i, r: (0, 0))],
        out_specs=pl.BlockSpec((1, tq, width), lambda i, r: (i, r, 0)),
        out_shape=jax.ShapeDtypeStruct((b, s, width), BF16),
        scratch_shapes=[pltpu.VMEM(bias_shape, F32)],
        compiler_params=_params("arbitrary", "arbitrary"),
        name="attn",
    )(q, k, v, kc, vc, rpb_rows)


def _merge_kernel(x_ref, ya_ref, yf_ref, mod_ref, g1_ref, wg_ref, wfo_ref, wao_ref, wo_ref,
                  o_ref, *, chunk):
    d = g1_ref.shape[1]
    m = mod_ref[0]

    def branches(c):
        rows = slice(c * chunk, (c + 1) * chunk)
        h = _rms_mod(x_ref[0, rows, :], g1_ref[...], m[0:1], m[1:2]).astype(BF16)
        return (_dot(h, wg_ref[...]), _dot(yf_ref[0, rows, :], wfo_ref[...]),
                _dot(ya_ref[0, rows, :], wao_ref[...]))

    def finish(c, gate_logits, four, attn):
        rows = slice(c * chunk, (c + 1) * chunk)
        gates = jax.nn.sigmoid(gate_logits)
        merged = gates[:, :d] * four + gates[:, d:] * attn
        o_ref[0, rows, :] = x_ref[0, rows, :] + m[2:3] * _dot(merged.astype(BF16), wo_ref[...])

    n_chunks = x_ref.shape[1] // chunk
    pending = branches(0)
    for c in range(n_chunks):
        upcoming = branches(c + 1) if c + 1 < n_chunks else None
        finish(c, *pending)
        pending = upcoming


def _merge(x, y_attn, y_four, mod, g1, w_gate, w_fo, w_ao, w_o, tm):
    b, s, d = x.shape
    const = lambda i, t: (0, 0)
    tok = lambda i, t: (i, t, 0)
    return pl.pallas_call(
        functools.partial(_merge_kernel, chunk=256),
        grid=(b, s // tm),
        in_specs=[pl.BlockSpec((1, tm, d), tok),
                  pl.BlockSpec((1, tm, NA_WIDTH), tok),
                  pl.BlockSpec((1, tm, FOURIER_WIDTH), tok),
                  pl.BlockSpec((1, 8, d), lambda i, t: (i, 0, 0)),
                  pl.BlockSpec((1, d), const),
                  pl.BlockSpec(w_gate.shape, const, pipeline_mode=pl.Buffered(1)),
                  pl.BlockSpec(w_fo.shape, const, pipeline_mode=pl.Buffered(1)),
                  pl.BlockSpec(w_ao.shape, const, pipeline_mode=pl.Buffered(1)),
                  pl.BlockSpec(w_o.shape, const, pipeline_mode=pl.Buffered(1))],
        out_specs=pl.BlockSpec((1, tm, d), tok),
        out_shape=jax.ShapeDtypeStruct(x.shape, F32),
        compiler_params=_params("arbitrary", "arbitrary"),
        name="merge",
    )(x, y_attn, y_four, mod, g1, w_gate, w_fo, w_ao, w_o)


def _mlp_kernel(x_ref, mod_ref, g2_ref, w1_ref, w2_ref, o_ref, *, ff_chunk):
    x = x_ref[0]
    m = mod_ref[0]
    h = _rms_mod(x, g2_ref[...], m[3:4], m[4:5]).astype(BF16)
    acc = jnp.zeros(x.shape, F32)
    for c in range(w1_ref.shape[1] // ff_chunk):
        a = jnp.maximum(_dot(h, w1_ref[:, c * ff_chunk:(c + 1) * ff_chunk]), 0.0)
        acc = acc + _dot((a * a).astype(BF16), w2_ref[c * ff_chunk:(c + 1) * ff_chunk, :])
    o_ref[0] = x + m[5:6] * acc


def _mlp(x, mod, g2, w1, w2, tm, ff_chunk):
    b, s, d = x.shape
    const = lambda i, t: (0, 0)
    return pl.pallas_call(
        functools.partial(_mlp_kernel, ff_chunk=ff_chunk),
        grid=(b, s // tm),
        in_specs=[pl.BlockSpec((1, tm, d), lambda i, t: (i, t, 0)),
                  pl.BlockSpec((1, 8, d), lambda i, t: (i, 0, 0)),
                  pl.BlockSpec((1, d), const),
                  pl.BlockSpec(w1.shape, const, pipeline_mode=pl.Buffered(1)),
                  pl.BlockSpec(w2.shape, const, pipeline_mode=pl.Buffered(1))],
        out_specs=pl.BlockSpec((1, tm, d), lambda i, t: (i, t, 0)),
        out_shape=jax.ShapeDtypeStruct(x.shape, F32),
        compiler_params=_params("arbitrary", "arbitrary"),
        name="mlp",
    )(x, mod, g2, w1, w2)


def _dft_tables(rows):
    n = rows * GRID_W
    gd = FOURIER_GROUP_DIM
    c = np.arange(gd)
    ang = 2.0 * np.pi * np.outer(c, c) / gd
    cs = np.concatenate([np.cos(ang), np.sin(ang)], axis=1) * 2.0 ** -3
    cs = np.kron(np.eye(2), cs)
    a = np.arange(rows)
    ang = 2.0 * np.pi * np.outer(a, a) / rows
    c1, s1 = np.cos(ang), np.sin(ang)
    w1 = np.block([[c1, -s1], [s1, c1]]) * 2.0 ** -4
    b = np.arange(GRID_W)
    k = a[:, None] + rows * b[None, :]
    ang = 2.0 * np.pi * k[:, :, None] * b[None, None, :] / n
    g2 = np.concatenate([np.cos(ang), -np.sin(ang)], axis=2) * 2.0 ** -3
    assert gd * rows * GRID_W == 2 ** 20
    return tuple(jnp.asarray(m, F32).astype(BF16) for m in (cs, w1, g2))


def _qk_lane_perm():
    lane = np.arange(NA_WIDTH)
    pair, in_block = lane // LANES, lane % LANES
    side, in_side = in_block // HALF_LANES, in_block % HALF_LANES
    head = pair * (LANES // HEAD_DIM) + in_side // HEAD_LANES
    axis, freq = (in_side % HEAD_LANES) // N_FREQ, in_side % N_FREQ
    return head * HEAD_DIM + axis * (2 * N_FREQ) + side * N_FREQ + freq


def _gain_lanes(g):
    sides = g.reshape(2, 2, N_FREQ).transpose(1, 0, 2).reshape(2, HEAD_LANES)
    block = jnp.concatenate([jnp.tile(sides[0], LANES // HEAD_DIM),
                             jnp.tile(sides[1], LANES // HEAD_DIM)])
    return jnp.tile(block, NA_WIDTH // LANES)[None, :]


def _rope_tables(seq):
    t = np.arange(seq)
    axis_dim = 2 * N_FREQ
    inv_freq = ROPE_THETA ** (-np.arange(0, axis_dim, 2, dtype=np.float64) / axis_dim)
    ang_r = (t // GRID_W)[:, None] * inv_freq
    ang_c = (t % GRID_W)[:, None] * inv_freq
    ang = np.tile(np.concatenate([ang_r, ang_c], axis=1), (1, LANES // HEAD_LANES))
    sign = np.repeat([-1.0, 1.0], HALF_LANES)
    return jnp.asarray(np.cos(ang), F32), jnp.asarray(np.sin(ang) * sign, F32)


def _rpb_rows(rpb):
    left = GRID_W - WIN_COLS
    padded = jnp.pad(rpb * LOG2_E, ((0, 0), (0, 0), (left, LANES - left - rpb.shape[-1])))
    return padded.reshape(rpb.shape[0] * rpb.shape[1], LANES)


def kernel(x, c, ctx, c_ctx, w_ada, b_ada, norm1_g, norm2_g, w_in, q_norm_g, k_norm_g, rpb,
           w_branch_gate, w_fourier_out, w_attn_out, w_out, w_mlp1, w_mlp2):
    b, s, d = x.shape
    assert w_ada.shape[0] == 1, "single-layer block only"
    n_rows = s // GRID_W
    assert n_rows >= WIN_ROWS and b < 8

    perm = _qk_lane_perm()
    head_id = perm[:NA_WIDTH // 2] // HEAD_DIM
    head_mean = jnp.asarray((head_id[:, None] == head_id[None, :]) / HEAD_DIM, F32).astype(BF16)
    perm_matrix = jnp.asarray(np.arange(NA_WIDTH)[:, None] == perm[None, :], F32).astype(BF16)

    cond = jnp.zeros((8, d), F32).at[:b].set(c).at[b].set(c_ctx)
    mods, w_in_b = _prep(cond, w_ada[0], b_ada, w_in[0], perm_matrix, steps=4)
    mod = jnp.pad(mods[:b].reshape(b, N_MOD, d), ((0, 0), (0, 8 - N_MOD), (0, 0)))
    mod_ctx = jnp.pad(mods[b].reshape(N_MOD, d), ((0, 8 - N_MOD), (0, 0)))

    qg = _gain_lanes(q_norm_g[0]) * (HEAD_DIM ** -0.5 * LOG2_E)
    kg = _gain_lanes(k_norm_g[0])
    cos, sin = _rope_tables(s)
    cs, w1, g2 = _dft_tables(n_rows)

    kc, vc = _ctx_kv(ctx, mod_ctx, norm1_g, w_in_b, kg, head_mean)
    f, q, k, v = _proj(x, mod, norm1_g, w_in_b, qg, kg, head_mean, cos, sin, tm=2048)
    y_four, (w_gate_b, w_fo_b, w_ao_b, w_o_b, w_mlp1_b, w_mlp2_b) = _fourier(
        f, cs, w1, g2, [w_branch_gate[0], w_fourier_out[0], w_attn_out[0], w_out[0],
                        w_mlp1[0], w_mlp2[0]])
    y_attn = _attn(q, k, v, kc, vc, _rpb_rows(rpb[0]), rows_per_step=16)
    x1 = _merge(x, y_attn, y_four, mod, norm1_g, w_gate_b, w_fo_b, w_ao_b, w_o_b, tm=1024)
    return _mlp(x1, mod, norm2_g, w_mlp1_b, w_mlp2_b, tm=1024, ff_chunk=1024)
```
